```python
import math
import jax, jax.numpy as jnp
from jax import lax
import numpy as np

D_MODEL = 1024
BATCH = 8
SEQ = 4096
DEPTH = 1

D_ATTN = D_MODEL // 2
D_LRU = D_MODEL - D_ATTN
DIFF_HEAD_DIM = 64
N_DIFF_HEADS = D_ATTN // (2 * DIFF_HEAD_DIM)
DIFF_V_DIM = 2 * DIFF_HEAD_DIM
Q_BLOCK = 128
NUM_BUCKETS = 32
MAX_DISTANCE = 128
LRU_BLOCKS = 8
LRU_BLOCK_DIM = D_LRU // LRU_BLOCKS
CONV_WIDTH = 4
LRU_C = 8.0
D_FF = int(math.ceil(D_MODEL * 8 / 3 / 256) * 256)
D_IN = 3 * D_ATTN + 2 * D_LRU
NORM_EPS = 1e-6

kernel_name = "hybrid_diffattn_rglru_encoder_layer"


def rms_norm(x, g):
    xf = x.astype(jnp.float32)
    y = xf * lax.rsqrt(jnp.mean(xf * xf, axis=-1, keepdims=True) + NORM_EPS)
    return (y * g.astype(jnp.float32)).astype(x.dtype)


def t5_bucket(rel):
    half = NUM_BUCKETS // 2
    ret = jnp.where(rel > 0, half, 0)
    n = jnp.abs(rel)
    max_exact = half // 2
    nf = jnp.maximum(n, 1).astype(jnp.float32)
    large = max_exact + (jnp.log(nf / max_exact) / math.log(MAX_DISTANCE / max_exact)
                         * (half - max_exact)).astype(jnp.int32)
    large = jnp.minimum(large, half - 1)
    return ret + jnp.where(n < max_exact, n, large)


def diff_attention(q, k, v, lam, rel_bias, subln_g, lambda_init):
    B, S, _ = q.shape
    H, Dh, Dv = N_DIFF_HEADS, DIFF_HEAD_DIM, DIFF_V_DIM
    nb = S // Q_BLOCK
    q = q.reshape(B, S, H, 2, Dh)
    k = k.reshape(B, S, H, 2, Dh)
    v = v.reshape(B, S, H, Dv).transpose(0, 2, 1, 3)
    k1 = k[:, :, :, 0].transpose(0, 2, 1, 3)
    k2 = k[:, :, :, 1].transpose(0, 2, 1, 3)
    q1b = q[:, :, :, 0].transpose(0, 2, 1, 3).reshape(B, H, nb, Q_BLOCK, Dh).transpose(2, 0, 1, 3, 4)
    q2b = q[:, :, :, 1].transpose(0, 2, 1, 3).reshape(B, H, nb, Q_BLOCK, Dh).transpose(2, 0, 1, 3, 4)
    starts = jnp.arange(nb, dtype=jnp.int32) * Q_BLOCK
    key_pos = jnp.arange(S, dtype=jnp.int32)
    scale = 1.0 / math.sqrt(Dh)
    table = rel_bias.astype(jnp.float32)

    def one_block(args):
        q1_blk, q2_blk, start = args
        q_pos = start + jnp.arange(Q_BLOCK, dtype=jnp.int32)
        bucket = t5_bucket(key_pos[None, :] - q_pos[:, None])
        bias = table[bucket].transpose(2, 0, 1)[None]
        s1 = jnp.einsum('bhqd,bhkd->bhqk', q1_blk, k1).astype(jnp.float32) * scale + bias
        s2 = jnp.einsum('bhqd,bhkd->bhqk', q2_blk, k2).astype(jnp.float32) * scale + bias
        attn = jax.nn.softmax(s1, axis=-1) - lam * jax.nn.softmax(s2, axis=-1)
        return jnp.einsum('bhqk,bhkd->bhqd', attn.astype(v.dtype), v)

    o = lax.map(one_block, (q1b, q2b, starts))
    o = o.transpose(1, 0, 3, 2, 4).reshape(B, S, H, Dv)
    o = rms_norm(o, subln_g) * (1.0 - lambda_init)
    return o.reshape(B, S, H * Dv)


def block_diag_linear(x, w, b):
    B, S, _ = x.shape
    xb = x.reshape(B, S, LRU_BLOCKS, LRU_BLOCK_DIM)
    return jnp.einsum('bsnd,nde->bsne', xb, w).reshape(B, S, D_LRU) + b


def centred_depthwise_conv(x, w, b):
    S = x.shape[1]
    xp = jnp.pad(x, ((0, 0), (CONV_WIDTH // 2, CONV_WIDTH - 1 - CONV_WIDTH // 2), (0, 0)))
    out = sum(xp[:, t:t + S] * w[t] for t in range(CONV_WIDTH))
    return out + b


def rglru_direction(xc, w_r, b_r, w_i, b_i, lam, reverse):
    r = jax.nn.sigmoid(block_diag_linear(xc, w_r, b_r))
    i = jax.nn.sigmoid(block_diag_linear(xc, w_i, b_i))
    log_a = -LRU_C * r * jax.nn.softplus(-lam)
    a = jnp.exp(log_a)
    u = jnp.sqrt(-jnp.expm1(2.0 * log_a)) * (i * xc)

    def combine(left, right):
        a1, b1 = left
        a2, b2 = right
        return a1 * a2, a2 * b1 + b2

    _, h = lax.associative_scan(combine, (a, u), reverse=reverse, axis=1)
    return h


def bidirectional_rglru_group(xr, gr, conv_w, conv_b, w_rg, b_rg, w_ig, b_ig, lru_lambda):
    xf = xr.astype(jnp.float32)
    xc = centred_depthwise_conv(xf, conv_w.astype(jnp.float32), conv_b.astype(jnp.float32))
    h_fwd = rglru_direction(xc, w_rg[0].astype(jnp.float32), b_rg[0].astype(jnp.float32),
                            w_ig[0].astype(jnp.float32), b_ig[0].astype(jnp.float32),
                            lru_lambda[0].astype(jnp.float32), reverse=False)
    h_bwd = rglru_direction(xc, w_rg[1].astype(jnp.float32), b_rg[1].astype(jnp.float32),
                            w_ig[1].astype(jnp.float32), b_ig[1].astype(jnp.float32),
                            lru_lambda[1].astype(jnp.float32), reverse=True)
    y = jax.nn.gelu(gr.astype(jnp.float32)) * (h_fwd + h_bwd)
    return y.astype(xr.dtype)


def setup_inputs(seed: int = 0) -> dict:
    key = jax.random.key(seed)
    ks = jax.random.split(key, 24)
    f32 = jnp.float32

    def nrm(k, shape, scale):
        return jax.random.normal(k, shape, f32) * scale

    def gain(k, shape):
        return 1.0 + 0.05 * jax.random.normal(k, shape, f32)

    u = jax.random.uniform(ks[13], (DEPTH, 2, D_LRU), f32, 0.9, 0.999)
    a0 = u ** (1.0 / LRU_C)
    lru_lambda = jnp.log(a0) - jnp.log1p(-a0)
    return {
        "x": jax.random.normal(ks[0], (BATCH, SEQ, D_MODEL), f32),
        "attn_norm_g": gain(ks[1], (DEPTH, D_MODEL)),
        "w_in": nrm(ks[2], (DEPTH, D_MODEL, D_IN), D_MODEL ** -0.5),
        "lambda_q1": nrm(ks[3], (DEPTH, DIFF_HEAD_DIM), 0.1),
        "lambda_k1": nrm(ks[4], (DEPTH, DIFF_HEAD_DIM), 0.1),
        "lambda_q2": nrm(ks[5], (DEPTH, DIFF_HEAD_DIM), 0.1),
        "lambda_k2": nrm(ks[6], (DEPTH, DIFF_HEAD_DIM), 0.1),
        "subln_g": gain(ks[7], (DEPTH, DIFF_V_DIM)),
        "rel_bias": nrm(ks[8], (NUM_BUCKETS, N_DIFF_HEADS), 0.5),
        "conv_w": nrm(ks[9], (DEPTH, CONV_WIDTH, D_LRU), CONV_WIDTH ** -0.5),
        "conv_b": nrm(ks[10], (DEPTH, D_LRU), 0.01),
        "w_rg": nrm(ks[11], (DEPTH, 2, LRU_BLOCKS, LRU_BLOCK_DIM, LRU_BLOCK_DIM), LRU_BLOCK_DIM ** -0.5),
        "b_rg": nrm(ks[12], (DEPTH, 2, D_LRU), 0.01),
        "w_ig": nrm(ks[14], (DEPTH, 2, LRU_BLOCKS, LRU_BLOCK_DIM, LRU_BLOCK_DIM), LRU_BLOCK_DIM ** -0.5),
        "b_ig": nrm(ks[15], (DEPTH, 2, D_LRU), 0.01),
        "lru_lambda": lru_lambda,
        "w_out": nrm(ks[16], (DEPTH, D_MODEL, D_MODEL), D_MODEL ** -0.5),
        "ffn_norm_g": gain(ks[17], (DEPTH, D_MODEL)),
        "w_gate": nrm(ks[18], (DEPTH, D_MODEL, D_FF), D_MODEL ** -0.5),
        "w_up": nrm(ks[19], (DEPTH, D_MODEL, D_FF), D_MODEL ** -0.5),
        "w_down": nrm(ks[20], (DEPTH, D_FF, D_MODEL), D_FF ** -0.5),
        "final_norm_g": gain(ks[21], (D_MODEL,)),
    }


def reference(x, attn_norm_g, w_in, lambda_q1, lambda_k1, lambda_q2, lambda_k2, subln_g,
              rel_bias, conv_w, conv_b, w_rg, b_rg, w_ig, b_ig, lru_lambda, w_out,
              ffn_norm_g, w_gate, w_up, w_down, final_norm_g):
    h = x
    for l in range(DEPTH):
        lambda_init = 0.8 - 0.6 * math.exp(-0.3 * l)
        n = rms_norm(h, attn_norm_g[l])
        proj = jnp.einsum('bsd,de->bse', n, w_in[l])
        q, k, v, xr, gr = jnp.split(
            proj, [D_ATTN, 2 * D_ATTN, 3 * D_ATTN, 3 * D_ATTN + D_LRU], axis=-1)
        lam = (jnp.exp(jnp.sum(lambda_q1[l].astype(jnp.float32) * lambda_k1[l].astype(jnp.float32)))
               - jnp.exp(jnp.sum(lambda_q2[l].astype(jnp.float32) * lambda_k2[l].astype(jnp.float32)))
               + lambda_init)
        y_attn = diff_attention(q, k, v, lam, rel_bias, subln_g[l], lambda_init)
        y_lru = bidirectional_rglru_group(xr, gr, conv_w[l], conv_b[l], w_rg[l], b_rg[l],
                                          w_ig[l], b_ig[l], lru_lambda[l])
        mixed = jnp.concatenate([y_attn, y_lru], axis=-1)
        h = h + jnp.einsum('bse,ed->bsd', mixed, w_out[l])
        n2 = rms_norm(h, ffn_norm_g[l])
        g = jnp.einsum('bsd,df->bsf', n2, w_gate[l])
        up = jnp.einsum('bsd,df->bsf', n2, w_up[l])
        h = h + jnp.einsum('bsf,fd->bsd', jax.nn.silu(g) * up, w_down[l])
    return rms_norm(h, final_norm_g)
```

```python
import functools
import math

import jax
import jax.numpy as jnp
from jax import lax
from jax.experimental import pallas as pl
from jax.experimental.pallas import tpu as pltpu

D_MODEL = 1024
D_ATTN = 512
D_LRU = 512
HEAD_DIM = 64
N_HEADS = 4
V_DIM = 128
NUM_BUCKETS = 32
LRU_BLOCK = 64
LRU_C = 8.0
CONV_WIDTH = 4
D_FF = 2816
NORM_EPS = 1e-6
LAMBDA_INIT = 0.2
LOG2E = 1.4426950408889634

LANES = 128
ATTN_TILE = 256
ONES_ROWS = 16
LRU_CHANNELS = 256
LRU_CHUNK = 512
FFN_CHUNK = 256
VMEM_LIMIT = 56 * 1024 * 1024

_BUCKET_THRESHOLDS = (12, 16, 23, 32, 46, 64, 91)
assert _BUCKET_THRESHOLDS[-1] <= ATTN_TILE


def _nt_dot(a, b):
    return lax.dot_general(a, b, (((1,), (1,)), ((), ())), preferred_element_type=jnp.float32)


def _dot(a, b):
    return jnp.dot(a, b, preferred_element_type=jnp.float32)


def _bias_kernel(table_ref, out_ref):
    h = pl.program_id(0)
    d = pl.program_id(1)
    t = ATTN_TILE
    kk = lax.broadcasted_iota(jnp.int32, (t, t), 0)
    qq = lax.broadcasted_iota(jnp.int32, (t, t), 1)
    rel = (d - 1) * t + kk - qq
    n = jnp.abs(rel)
    large = jnp.full((t, t), NUM_BUCKETS // 4, jnp.int32)
    for thr in _BUCKET_THRESHOLDS:
        large = large + (n >= thr).astype(jnp.int32)
    bucket = jnp.where(rel > 0, NUM_BUCKETS // 2, 0) + jnp.where(n < NUM_BUCKETS // 4, n, large)
    acc = jnp.zeros((t, t), jnp.float32)
    for b in range(NUM_BUCKETS):
        acc = jnp.where(bucket == b, table_ref[b, h], acc)
    out_ref[0, 0] = acc * LOG2E


def _bias_tiles(rel_bias):
    t = ATTN_TILE
    return pl.pallas_call(
        _bias_kernel,
        grid=(N_HEADS, 3),
        in_specs=[pl.BlockSpec(memory_space=pltpu.SMEM)],
        out_specs=pl.BlockSpec((1, 1, t, t), lambda h, d: (h, d, 0, 0)),
        out_shape=jax.ShapeDtypeStruct((N_HEADS, 3, t, t), jnp.float32),
        name="bias_tiles",
    )(rel_bias)


def _inproj_kernel(x_ref, g_ref, wqt_ref, wk_ref, wvt_ref, wx_ref, wg_ref,
                   qt_ref, k_ref, vt_ref, xr_ref, gr_ref):
    x = x_ref[0]
    ms = jnp.mean(x * x, axis=-1, keepdims=True)
    n = (x * lax.rsqrt(ms + NORM_EPS) * g_ref[...]).astype(jnp.bfloat16)
    qt = _nt_dot(wqt_ref[...], n)
    qt_ref[0] = (qt * (LOG2E / math.sqrt(HEAD_DIM))).astype(jnp.bfloat16)
    k_ref[0] = _dot(n, wk_ref[...]).astype(jnp.bfloat16)
    vt = _nt_dot(wvt_ref[...], n).astype(jnp.bfloat16)
    for c in range(vt_ref.shape[1]):
        vt_ref[0, c] = vt[:, c * ATTN_TILE:(c + 1) * ATTN_TILE]
    xr_ref[0] = _dot(n, wx_ref[...])
    gr_ref[0] = _dot(n, wg_ref[...])


def _inproj(x, g, wqt, wk, wvt, wx, wg, tm=512):
    b, s, d = x.shape
    nc = s // ATTN_TILE
    cpt = tm // ATTN_TILE
    const = lambda shape: pl.BlockSpec(shape, lambda bi, ti: (0,) * len(shape))
    return pl.pallas_call(
        _inproj_kernel,
        grid=(b, s // tm),
        in_specs=[
            pl.BlockSpec((1, tm, d), lambda bi, ti: (bi, ti, 0)),
            const((1, d)),
            const((D_ATTN, d)), const((d, D_ATTN)), const((D_ATTN, d)),
            const((d, D_LRU)), const((d, D_LRU)),
        ],
        out_specs=[
            pl.BlockSpec((1, D_ATTN, tm), lambda bi, ti: (bi, 0, ti)),
            pl.BlockSpec((1, tm, D_ATTN), lambda bi, ti: (bi, ti, 0)),
            pl.BlockSpec((1, cpt, D_ATTN, ATTN_TILE), lambda bi, ti: (bi, ti, 0, 0)),
            pl.BlockSpec((1, tm, D_LRU), lambda bi, ti: (bi, ti, 0)),
            pl.BlockSpec((1, tm, D_LRU), lambda bi, ti: (bi, ti, 0)),
        ],
        out_shape=[
            jax.ShapeDtypeStruct((b, D_ATTN, s), jnp.bfloat16),
            jax.ShapeDtypeStruct((b, s, D_ATTN), jnp.bfloat16),
            jax.ShapeDtypeStruct((b, nc, D_ATTN, ATTN_TILE), jnp.bfloat16),
            jax.ShapeDtypeStruct((b, s, D_LRU), jnp.float32),
            jax.ShapeDtypeStruct((b, s, D_LRU), jnp.float32),
        ],
        compiler_params=pltpu.CompilerParams(
            dimension_semantics=("arbitrary", "arbitrary"), vmem_limit_bytes=VMEM_LIMIT),
        name="inproj",
    )(x, g, wqt, wk, wvt, wx, wg)


def _attn_kernel(table_ref, lamv_ref, g_ref, qt_ref, k_ref, vt_ref, bias_ref, o_ref,
                 m_ref, acc_ref, *, n_chunks):
    t = ATTN_TILE
    h = pl.program_id(1)
    i = pl.program_id(2)

    qt = qt_ref[0]
    z = jnp.zeros((HEAD_DIM, t), jnp.bfloat16)
    qz = jnp.concatenate([jnp.concatenate([qt[:HEAD_DIM], z], axis=1),
                          jnp.concatenate([z, qt[HEAD_DIM:]], axis=1)], axis=0)
    ones = jnp.ones((ONES_ROWS, t), jnp.bfloat16)
    m_ref[...] = jnp.full(m_ref.shape, -1e30, jnp.float32)
    acc_ref[...] = jnp.zeros(acc_ref.shape, jnp.float32)

    def step(j, bias, const):
        kc = k_ref[0, pl.ds(pl.multiple_of(j * t, t), t), :]
        s = _dot(kc, qz)
        if bias is not None:
            s = s + jnp.concatenate([bias, bias], axis=1)
        m_old = m_ref[...]
        m_cur = jnp.max(s, axis=0, keepdims=True)
        if const is not None:
            m_cur = m_cur + const
        m_new = jnp.maximum(m_old, m_cur)
        alpha = jnp.exp2(m_old - m_new)
        shift = m_new if const is None else m_new - const
        p = jnp.exp2(s - shift).astype(jnp.bfloat16)
        vta = jnp.concatenate([vt_ref[0, j], ones], axis=0)
        acc_ref[...] = acc_ref[...] * alpha + _dot(vta, p)
        m_ref[...] = m_new

    c_left = table_ref[NUM_BUCKETS // 2 - 1, h] * LOG2E
    c_right = table_ref[NUM_BUCKETS - 1, h] * LOG2E

    def left_body(j, carry):
        step(j, None, c_left)
        return carry

    def right_body(j, carry):
        step(j, None, c_right)
        return carry

    lax.fori_loop(0, jnp.maximum(i - 1, 0), left_body, 0)

    @pl.when(i >= 1)
    def _():
        step(i - 1, bias_ref[0, 0], None)

    step(i, bias_ref[0, 1], None)

    @pl.when(i + 1 < n_chunks)
    def _():
        step(i + 1, bias_ref[0, 2], None)

    lax.fori_loop(i + 2, n_chunks, right_body, 0)

    lv = lamv_ref[...]
    lam = (jnp.exp(jnp.sum(lv[0:1] * lv[1:2], axis=-1, keepdims=True))
           - jnp.exp(jnp.sum(lv[2:3] * lv[3:4], axis=-1, keepdims=True)) + LAMBDA_INIT)
    acc = acc_ref[...]
    o1 = acc[:V_DIM, :t] / acc[V_DIM:V_DIM + 1, :t]
    o2 = acc[:V_DIM, t:] / acc[V_DIM:V_DIM + 1, t:]
    o = o1 - lam * o2
    ms = jnp.mean(o * o, axis=0, keepdims=True)
    y = (o * lax.rsqrt(ms + NORM_EPS)).T
    o_ref[0] = (y * g_ref[...] * (1.0 - LAMBDA_INIT)).astype(o_ref.dtype)


def _attention(rel_bias, lamv, subln_g, qt, k, vt, bias):
    b, _, s = qt.shape
    t = ATTN_TILE
    nq = s // t
    smem = pl.BlockSpec(memory_space=pltpu.SMEM)
    return pl.pallas_call(
        functools.partial(_attn_kernel, n_chunks=nq),
        grid=(b, N_HEADS, nq),
        in_specs=[
            smem,
            pl.BlockSpec((4, HEAD_DIM), lambda bi, hi, qi: (0, 0)),
            pl.BlockSpec((1, V_DIM), lambda bi, hi, qi: (0, 0)),
            pl.BlockSpec((1, V_DIM, t), lambda bi, hi, qi: (bi, hi, qi)),
            pl.BlockSpec((1, s, V_DIM), lambda bi, hi, qi: (bi, 0, hi)),
            pl.BlockSpec((1, nq, V_DIM, t), lambda bi, hi, qi: (bi, 0, hi, 0)),
            pl.BlockSpec((1, 3, t, t), lambda bi, hi, qi: (hi, 0, 0, 0)),
        ],
        out_specs=pl.BlockSpec((1, t, V_DIM), lambda bi, hi, qi: (bi, qi, hi)),
        out_shape=jax.ShapeDtypeStruct((b, s, D_ATTN), jnp.bfloat16),
        scratch_shapes=[pltpu.VMEM((1, 2 * t), jnp.float32),
                        pltpu.VMEM((V_DIM + ONES_ROWS, 2 * t), jnp.float32)],
        compiler_params=pltpu.CompilerParams(
            dimension_semantics=("arbitrary", "arbitrary", "arbitrary"),
            vmem_limit_bytes=VMEM_LIMIT),
        name="diff_attention",
    )(rel_bias, lamv, subln_g, qt, k, vt, bias)


def _lru_kernel(xr_ref, gr_ref, cw_ref, cb_ref, wg_ref, bg_ref, lam_ref, o_ref,
                xp_ref, af_ref, uf_ref, ab_ref, ub_ref, *, seq):
    cb = LRU_CHANNELS
    pad = 8
    xp_ref[0:pad, :] = jnp.zeros((pad, cb), jnp.float32)
    xp_ref[pad + seq:2 * pad + seq, :] = jnp.zeros((pad, cb), jnp.float32)
    xp_ref[pad:pad + seq, :] = xr_ref[0]

    cw = cw_ref[...]
    cbias = cb_ref[...]
    sp = jnp.log1p(jnp.exp(-lam_ref[...]))
    n = LRU_CHUNK

    def gate_chunk(ci, carry):
        t0 = pl.multiple_of(ci * n, n)
        blk = xp_ref[pl.ds(t0, n + 2 * pad), :]
        xc = cbias
        for tap in range(CONV_WIDTH):
            lo = pad - CONV_WIDTH // 2 + tap
            xc = xc + blk[lo:lo + n] * cw[tap:tap + 1]
        for g in range(cb // LANES):
            cols = slice(g * LANES, (g + 1) * LANES)
            xg = xc[:, cols]
            gates = _dot(xg.astype(jnp.bfloat16), wg_ref[g]) + bg_ref[g]
            for direction, (a_ref, u_ref) in enumerate(((af_ref, uf_ref), (ab_ref, ub_ref))):
                r = jax.nn.sigmoid(gates[:, (2 * direction) * LANES:(2 * direction + 1) * LANES])
                ig = jax.nn.sigmoid(gates[:, (2 * direction + 1) * LANES:(2 * direction + 2) * LANES])
                log_a = (-LRU_C) * r * sp[direction:direction + 1, cols]
                a_ref[pl.ds(t0, n), cols] = jnp.exp(log_a)
                u_ref[pl.ds(t0, n), cols] = jnp.sqrt(1.0 - jnp.exp(2.0 * log_a)) * (ig * xg)
        return carry

    lax.fori_loop(0, seq // n, gate_chunk, 0)

    def scan_body(t, carry):
        hf, hb = carry
        tb = seq - 1 - t
        hf = af_ref[pl.ds(t, 1), :] * hf + uf_ref[pl.ds(t, 1), :]
        uf_ref[pl.ds(t, 1), :] = hf
        hb = ab_ref[pl.ds(tb, 1), :] * hb + ub_ref[pl.ds(tb, 1), :]
        ub_ref[pl.ds(tb, 1), :] = hb
        return hf, hb

    zero = jnp.zeros((1, cb), jnp.float32)
    lax.fori_loop(0, seq, scan_body, (zero, zero), unroll=8)

    def out_chunk(ci, carry):
        t0 = pl.multiple_of(ci * n, n)
        rows = pl.ds(t0, n)
        gate = jax.nn.gelu(gr_ref[0, rows, :])
        o_ref[0, rows, :] = (gate * (uf_ref[rows, :] + ub_ref[rows, :])).astype(o_ref.dtype)
        return carry

    lax.fori_loop(0, seq // n, out_chunk, 0)


def _rglru(xr, gr, conv_w, conv_b, wg, bg, lru_lambda):
    b, s, c = xr.shape
    cb = LRU_CHANNELS
    groups = cb // LANES
    seq_spec = pl.BlockSpec((1, s, cb), lambda bi, ci: (bi, 0, ci))
    return pl.pallas_call(
        functools.partial(_lru_kernel, seq=s),
        grid=(b, c // cb),
        in_specs=[
            seq_spec, seq_spec,
            pl.BlockSpec((CONV_WIDTH, cb), lambda bi, ci: (0, ci)),
            pl.BlockSpec((1, cb), lambda bi, ci: (0, ci)),
            pl.BlockSpec((groups, LANES, 4 * LANES), lambda bi, ci: (ci, 0, 0)),
            pl.BlockSpec((groups, 1, 4 * LANES), lambda bi, ci: (ci, 0, 0)),
            pl.BlockSpec((2, cb), lambda bi, ci: (0, ci)),
        ],
        out_specs=seq_spec,
        out_shape=jax.ShapeDtypeStruct((b, s, c), jnp.bfloat16),
        scratch_shapes=[pltpu.VMEM((s + 16, cb), jnp.float32)]
        + [pltpu.VMEM((s, cb), jnp.float32) for _ in range(4)],
        compiler_params=pltpu.CompilerParams(
            dimension_semantics=("arbitrary", "arbitrary"), vmem_limit_bytes=VMEM_LIMIT),
        name="rglru",
    )(xr, gr, conv_w, conv_b, wg, bg, lru_lambda)


def _lru_gate_weights(w_rg, b_rg, w_ig, b_ig):
    n_groups = D_LRU // LANES
    per = LANES // LRU_BLOCK

    def blockdiag(w):
        w = w.reshape(n_groups, per, LRU_BLOCK, LRU_BLOCK)
        eye = jnp.eye(per, dtype=w.dtype)
        return jnp.einsum('gpde,pq->gpdqe', w, eye).reshape(n_groups, LANES, LANES)

    wg = jnp.concatenate([blockdiag(w_rg[0]), blockdiag(w_ig[0]),
                          blockdiag(w_rg[1]), blockdiag(w_ig[1])], axis=-1)
    bg = jnp.concatenate([b.reshape(n_groups, 1, LANES) for b in (b_rg[0], b_ig[0], b_rg[1], b_ig[1])],
                         axis=-1)
    return wg.astype(jnp.bfloat16), bg


def _ffn_kernel(x_ref, ya_ref, yl_ref, wo_ref, g2_ref, wgate_ref, wup_ref, wdown_ref, gf_ref, o_ref,
                act_ref):
    h1 = x_ref[...] + _dot(ya_ref[...], wo_ref[:D_ATTN, :]) + _dot(yl_ref[...], wo_ref[D_ATTN:, :])
    ms = jnp.mean(h1 * h1, axis=-1, keepdims=True)
    n2 = (h1 * lax.rsqrt(ms + NORM_EPS) * g2_ref[...]).astype(jnp.bfloat16)
    for c in range(D_FF // FFN_CHUNK):
        cols = slice(c * FFN_CHUNK, (c + 1) * FFN_CHUNK)
        gate = _dot(n2, wgate_ref[:, cols])
        up = _dot(n2, wup_ref[:, cols])
        act_ref[:, cols] = (gate * jax.nn.sigmoid(gate) * up).astype(jnp.bfloat16)
    h2 = h1 + _dot(act_ref[...], wdown_ref[...])
    ms = jnp.mean(h2 * h2, axis=-1, keepdims=True)
    o_ref[...] = h2 * lax.rsqrt(ms + NORM_EPS) * gf_ref[...]


def _outproj_ffn(x2, ya, yl, wo, g2, wgate, wup, wdown, gf, tm=512):
    n_tok, d = x2.shape
    once = lambda shape: pl.BlockSpec(shape, lambda ti: (0,) * len(shape),
                                      pipeline_mode=pl.Buffered(1))
    return pl.pallas_call(
        _ffn_kernel,
        grid=(n_tok // tm,),
        in_specs=[
            pl.BlockSpec((tm, d), lambda ti: (ti, 0)),
            pl.BlockSpec((tm, D_ATTN), lambda ti: (ti, 0)),
            pl.BlockSpec((tm, D_LRU), lambda ti: (ti, 0)),
            once((d, d)), once((1, d)),
            once((d, D_FF)), once((d, D_FF)), once((D_FF, d)), once((1, d)),
        ],
        out_specs=pl.BlockSpec((tm, d), lambda ti: (ti, 0)),
        out_shape=jax.ShapeDtypeStruct((n_tok, d), jnp.float32),
        scratch_shapes=[pltpu.VMEM((tm, D_FF), jnp.bfloat16)],
        compiler_params=pltpu.CompilerParams(
            dimension_semantics=("arbitrary",), vmem_limit_bytes=VMEM_LIMIT),
        name="outproj_ffn",
    )(x2, ya, yl, wo, g2, wgate, wup, wdown, gf)


def kernel(x, attn_norm_g, w_in, lambda_q1, lambda_k1, lambda_q2, lambda_k2, subln_g, rel_bias,
           conv_w, conv_b, w_rg, b_rg, w_ig, b_ig, lru_lambda, w_out, ffn_norm_g, w_gate, w_up,
           w_down, final_norm_g):
    b, s, d = x.shape
    bf16 = jnp.bfloat16
    w = w_in[0]
    wqt = w[:, :D_ATTN].T.astype(bf16)
    wk = w[:, D_ATTN:2 * D_ATTN].astype(bf16)
    wvt = w[:, 2 * D_ATTN:3 * D_ATTN].T.astype(bf16)
    wx = w[:, 3 * D_ATTN:3 * D_ATTN + D_LRU].astype(bf16)
    wgr = w[:, 3 * D_ATTN + D_LRU:].astype(bf16)

    qt, k, vt, xr, gr = _inproj(x, attn_norm_g[0][None, :], wqt, wk, wvt, wx, wgr)

    bias = _bias_tiles(rel_bias)
    lamv = jnp.stack([lambda_q1[0], lambda_k1[0], lambda_q2[0], lambda_k2[0]])
    y_attn = _attention(rel_bias, lamv, subln_g[0][None, :], qt, k, vt, bias)

    wg, bg = _lru_gate_weights(w_rg[0], b_rg[0], w_ig[0], b_ig[0])
    y_lru = _rglru(xr, gr, conv_w[0], conv_b[0][None, :], wg, bg, lru_lambda[0])

    out = _outproj_ffn(
        x.reshape(b * s, d), y_attn.reshape(b * s, D_ATTN), y_lru.reshape(b * s, D_LRU),
        w_out[0].astype(bf16), ffn_norm_g[0][None, :], w_gate[0].astype(bf16),
        w_up[0].astype(bf16), w_down[0].astype(bf16), final_norm_g[None, :])
    return out.reshape(b, s, d)
```

```python
import functools
import math

import jax
import jax.numpy as jnp
from jax import lax
from jax.experimental import pallas as pl
from jax.experimental.pallas import tpu as pltpu

D_MODEL = 1024
D_ATTN = 512
D_LRU = 512
HEAD_DIM = 64
N_HEADS = 4
V_DIM = 128
NUM_BUCKETS = 32
LRU_BLOCK = 64
LRU_C = 8.0
CONV_WIDTH = 4
D_FF = 2816
NORM_EPS = 1e-6
LAMBDA_INIT = 0.2
LOG2E = 1.4426950408889634

LANES = 128
ATTN_TILE = 256
ONES_ROWS = 16
LRU_CHANNELS = 256
LRU_CHUNK = 512
FFN_CHUNK = 256
VMEM_LIMIT = 56 * 1024 * 1024

_BUCKET_THRESHOLDS = (12, 16, 23, 32, 46, 64, 91)
assert _BUCKET_THRESHOLDS[-1] <= ATTN_TILE


def _nt_dot(a, b):
    return lax.dot_general(a, b, (((1,), (1,)), ((), ())), preferred_element_type=jnp.float32)


def _dot(a, b):
    return jnp.dot(a, b, preferred_element_type=jnp.float32)


def _bias_kernel(table_ref, out_ref):
    h = pl.program_id(0)
    d = pl.program_id(1)
    t = ATTN_TILE
    kk = lax.broadcasted_iota(jnp.int32, (t, t), 0)
    qq = lax.broadcasted_iota(jnp.int32, (t, t), 1)
    rel = (d - 1) * t + kk - qq
    n = jnp.abs(rel)
    large = jnp.full((t, t), NUM_BUCKETS // 4, jnp.int32)
    for thr in _BUCKET_THRESHOLDS:
        large = large + (n >= thr).astype(jnp.int32)
    bucket = jnp.where(rel > 0, NUM_BUCKETS // 2, 0) + jnp.where(n < NUM_BUCKETS // 4, n, large)
    acc = jnp.zeros((t, t), jnp.float32)
    for b in range(NUM_BUCKETS):
        acc = jnp.where(bucket == b, table_ref[b, h], acc)
    out_ref[0, 0] = acc * LOG2E


def _bias_tiles(rel_bias):
    t = ATTN_TILE
    return pl.pallas_call(
        _bias_kernel,
        grid=(N_HEADS, 3),
        in_specs=[pl.BlockSpec(memory_space=pltpu.SMEM)],
        out_specs=pl.BlockSpec((1, 1, t, t), lambda h, d: (h, d, 0, 0)),
        out_shape=jax.ShapeDtypeStruct((N_HEADS, 3, t, t), jnp.float32),
        name="bias_tiles",
    )(rel_bias)


def _inproj_kernel(x_ref, g_ref, wqt_ref, wk_ref, wvt_ref, wx_ref, wg_ref,
                   qt_ref, k_ref, vt_ref, xr_ref, gr_ref):
    x = x_ref[0]
    ms = jnp.mean(x * x, axis=-1, keepdims=True)
    n = (x * lax.rsqrt(ms + NORM_EPS) * g_ref[...]).astype(jnp.bfloat16)
    qt = _nt_dot(wqt_ref[...], n)
    qt_ref[0] = (qt * (LOG2E / math.sqrt(HEAD_DIM))).astype(jnp.bfloat16)
    k_ref[0] = _dot(n, wk_ref[...]).astype(jnp.bfloat16)
    vt = _nt_dot(wvt_ref[...], n).astype(jnp.bfloat16)
    for c in range(vt_ref.shape[1]):
        vt_ref[0, c] = vt[:, c * ATTN_TILE:(c + 1) * ATTN_TILE]
    xr_ref[0] = _dot(n, wx_ref[...])
    gr_ref[0] = _dot(n, wg_ref[...])


def _inproj(x, g, wqt, wk, wvt, wx, wg, tm=512):
    b, s, d = x.shape
    nc = s // ATTN_TILE
    cpt = tm // ATTN_TILE
    const = lambda shape: pl.BlockSpec(shape, lambda bi, ti: (0,) * len(shape))
    return pl.pallas_call(
        _inproj_kernel,
        grid=(b, s // tm),
        in_specs=[
            pl.BlockSpec((1, tm, d), lambda bi, ti: (bi, ti, 0)),
            const((1, d)),
            const((D_ATTN, d)), const((d, D_ATTN)), const((D_ATTN, d)),
            const((d, D_LRU)), const((d, D_LRU)),
        ],
        out_specs=[
            pl.BlockSpec((1, D_ATTN, tm), lambda bi, ti: (bi, 0, ti)),
            pl.BlockSpec((1, tm, D_ATTN), lambda bi, ti: (bi, ti, 0)),
            pl.BlockSpec((1, cpt, D_ATTN, ATTN_TILE), lambda bi, ti: (bi, ti, 0, 0)),
            pl.BlockSpec((1, tm, D_LRU), lambda bi, ti: (bi, ti, 0)),
            pl.BlockSpec((1, tm, D_LRU), lambda bi, ti: (bi, ti, 0)),
        ],
        out_shape=[
            jax.ShapeDtypeStruct((b, D_ATTN, s), jnp.bfloat16),
            jax.ShapeDtypeStruct((b, s, D_ATTN), jnp.bfloat16),
            jax.ShapeDtypeStruct((b, nc, D_ATTN, ATTN_TILE), jnp.bfloat16),
            jax.ShapeDtypeStruct((b, s, D_LRU), jnp.float32),
            jax.ShapeDtypeStruct((b, s, D_LRU), jnp.float32),
        ],
        compiler_params=pltpu.CompilerParams(
            dimension_semantics=("arbitrary", "arbitrary"), vmem_limit_bytes=VMEM_LIMIT),
        name="inproj",
    )(x, g, wqt, wk, wvt, wx, wg)


def _attn_kernel(table_ref, lamv_ref, g_ref, qt_ref, k_ref, vt_ref, bias_ref, o_ref,
                 m_ref, acc_ref, *, n_chunks):
    t = ATTN_TILE
    i = pl.program_id(1)
    heads = range(N_HEADS)

    z = jnp.zeros((HEAD_DIM, t), jnp.bfloat16)
    qz = []
    for h in heads:
        qt = qt_ref[0, h * V_DIM:(h + 1) * V_DIM, :]
        qz.append(jnp.concatenate([jnp.concatenate([qt[:HEAD_DIM], z], axis=1),
                                   jnp.concatenate([z, qt[HEAD_DIM:]], axis=1)], axis=0))
    ones = jnp.ones((ONES_ROWS, t), jnp.bfloat16)
    m_ref[...] = jnp.full(m_ref.shape, -1e30, jnp.float32)
    acc_ref[...] = jnp.zeros(acc_ref.shape, jnp.float32)

    def step(j, band, consts):
        rows = pl.ds(pl.multiple_of(j * t, t), t)

        def scores(h):
            s = _dot(k_ref[0, rows, h * V_DIM:(h + 1) * V_DIM], qz[h])
            if band is not None:
                bias = bias_ref[h, band]
                s = s + jnp.concatenate([bias, bias], axis=1)
            return s

        def accumulate(h, s):
            m_old = m_ref[h]
            m_cur = jnp.max(s, axis=0, keepdims=True)
            if consts is not None:
                m_cur = m_cur + consts[h]
            m_new = jnp.maximum(m_old, m_cur)
            alpha = jnp.exp2(m_old - m_new)
            shift = m_new if consts is None else m_new - consts[h]
            p = jnp.exp2(s - shift).astype(jnp.bfloat16)
            vta = jnp.concatenate([vt_ref[0, j, h * V_DIM:(h + 1) * V_DIM, :], ones], axis=0)
            acc_ref[h] = acc_ref[h] * alpha + _dot(vta, p)
            m_ref[h] = m_new

        s = scores(0)
        for h in heads[1:]:
            s_next = scores(h)
            accumulate(h - 1, s)
            s = s_next
        accumulate(N_HEADS - 1, s)

    c_left = [table_ref[NUM_BUCKETS // 2 - 1, h] * LOG2E for h in heads]
    c_right = [table_ref[NUM_BUCKETS - 1, h] * LOG2E for h in heads]

    def left_body(j, carry):
        step(j, None, c_left)
        return carry

    def right_body(j, carry):
        step(j, None, c_right)
        return carry

    lax.fori_loop(0, jnp.maximum(i - 1, 0), left_body, 0)

    @pl.when(i >= 1)
    def _():
        step(i - 1, 0, None)

    step(i, 1, None)

    @pl.when(i + 1 < n_chunks)
    def _():
        step(i + 1, 2, None)

    lax.fori_loop(jnp.minimum(i + 2, n_chunks), n_chunks, right_body, 0)

    lv = lamv_ref[...]
    lam = (jnp.exp(jnp.sum(lv[0:1] * lv[1:2], axis=-1, keepdims=True))
           - jnp.exp(jnp.sum(lv[2:3] * lv[3:4], axis=-1, keepdims=True)) + LAMBDA_INIT)
    for h in heads:
        acc = acc_ref[h]
        o1 = acc[:V_DIM, :t] / acc[V_DIM:V_DIM + 1, :t]
        o2 = acc[:V_DIM, t:] / acc[V_DIM:V_DIM + 1, t:]
        o = o1 - lam * o2
        ms = jnp.mean(o * o, axis=0, keepdims=True)
        y = (o * lax.rsqrt(ms + NORM_EPS)).T
        o_ref[0, :, h * V_DIM:(h + 1) * V_DIM] = (y * g_ref[...] * (1.0 - LAMBDA_INIT)).astype(o_ref.dtype)


def _attention(rel_bias, lamv, subln_g, qt, k, vt, bias):
    b, _, s = qt.shape
    t = ATTN_TILE
    nq = s // t
    smem = pl.BlockSpec(memory_space=pltpu.SMEM)
    return pl.pallas_call(
        functools.partial(_attn_kernel, n_chunks=nq),
        grid=(b, nq),
        in_specs=[
            smem,
            pl.BlockSpec((4, HEAD_DIM), lambda bi, qi: (0, 0)),
            pl.BlockSpec((1, V_DIM), lambda bi, qi: (0, 0)),
            pl.BlockSpec((1, D_ATTN, t), lambda bi, qi: (bi, 0, qi)),
            pl.BlockSpec((1, s, D_ATTN), lambda bi, qi: (bi, 0, 0)),
            pl.BlockSpec((1, nq, D_ATTN, t), lambda bi, qi: (bi, 0, 0, 0)),
            pl.BlockSpec((N_HEADS, 3, t, t), lambda bi, qi: (0, 0, 0, 0)),
        ],
        out_specs=pl.BlockSpec((1, t, D_ATTN), lambda bi, qi: (bi, qi, 0)),
        out_shape=jax.ShapeDtypeStruct((b, s, D_ATTN), jnp.bfloat16),
        scratch_shapes=[pltpu.VMEM((N_HEADS, 1, 2 * t), jnp.float32),
                        pltpu.VMEM((N_HEADS, V_DIM + ONES_ROWS, 2 * t), jnp.float32)],
        compiler_params=pltpu.CompilerParams(
            dimension_semantics=("arbitrary", "arbitrary"),
            vmem_limit_bytes=VMEM_LIMIT),
        name="diff_attention",
    )(rel_bias, lamv, subln_g, qt, k, vt, bias)


def _lru_kernel(xr_ref, gr_ref, cw_ref, cb_ref, wg_ref, bg_ref, lam_ref, o_ref,
                xp_ref, af_ref, uf_ref, ab_ref, ub_ref, *, seq):
    cb = LRU_CHANNELS
    pad = 8
    xp_ref[0:pad, :] = jnp.zeros((pad, cb), jnp.float32)
    xp_ref[pad + seq:2 * pad + seq, :] = jnp.zeros((pad, cb), jnp.float32)
    xp_ref[pad:pad + seq, :] = xr_ref[0]

    cw = cw_ref[...]
    cbias = cb_ref[...]
    sp = jnp.log1p(jnp.exp(-lam_ref[...]))
    n = LRU_CHUNK

    def gate_chunk(ci, carry):
        t0 = pl.multiple_of(ci * n, n)
        blk = xp_ref[pl.ds(t0, n + 2 * pad), :]
        xc = cbias
        for tap in range(CONV_WIDTH):
            lo = pad - CONV_WIDTH // 2 + tap
            xc = xc + blk[lo:lo + n] * cw[tap:tap + 1]
        for g in range(cb // LANES):
            cols = slice(g * LANES, (g + 1) * LANES)
            xg = xc[:, cols]
            gates = _dot(xg.astype(jnp.bfloat16), wg_ref[g]) + bg_ref[g]
            for direction, (a_ref, u_ref) in enumerate(((af_ref, uf_ref), (ab_ref, ub_ref))):
                r = jax.nn.sigmoid(gates[:, (2 * direction) * LANES:(2 * direction + 1) * LANES])
                ig = jax.nn.sigmoid(gates[:, (2 * direction + 1) * LANES:(2 * direction + 2) * LANES])
                log_a = (-LRU_C) * r * sp[direction:direction + 1, cols]
                a_ref[pl.ds(t0, n), cols] = jnp.exp(log_a)
                u_ref[pl.ds(t0, n), cols] = jnp.sqrt(1.0 - jnp.exp(2.0 * log_a)) * (ig * xg)
        return carry

    lax.fori_loop(0, seq // n, gate_chunk, 0)

    def scan_body(t, carry):
        hf, hb = carry
        tb = seq - 1 - t
        hf = af_ref[pl.ds(t, 1), :] * hf + uf_ref[pl.ds(t, 1), :]
        uf_ref[pl.ds(t, 1), :] = hf
        hb = ab_ref[pl.ds(tb, 1), :] * hb + ub_ref[pl.ds(tb, 1), :]
        ub_ref[pl.ds(tb, 1), :] = hb
        return hf, hb

    zero = jnp.zeros((1, cb), jnp.float32)
    lax.fori_loop(0, seq, scan_body, (zero, zero), unroll=8)

    def out_chunk(ci, carry):
        t0 = pl.multiple_of(ci * n, n)
        rows = pl.ds(t0, n)
        gate = jax.nn.gelu(gr_ref[0, rows, :])
        o_ref[0, rows, :] = (gate * (uf_ref[rows, :] + ub_ref[rows, :])).astype(o_ref.dtype)
        return carry

    lax.fori_loop(0, seq // n, out_chunk, 0)


def _rglru(xr, gr, conv_w, conv_b, wg, bg, lru_lambda):
    b, s, c = xr.shape
    cb = LRU_CHANNELS
    groups = cb // LANES
    seq_spec = pl.BlockSpec((1, s, cb), lambda bi, ci: (bi, 0, ci))
    return pl.pallas_call(
        functools.partial(_lru_kernel, seq=s),
        grid=(b, c // cb),
        in_specs=[
            seq_spec, seq_spec,
            pl.BlockSpec((CONV_WIDTH, cb), lambda bi, ci: (0, ci)),
            pl.BlockSpec((1, cb), lambda bi, ci: (0, ci)),
            pl.BlockSpec((groups, LANES, 4 * LANES), lambda bi, ci: (ci, 0, 0)),
            pl.BlockSpec((groups, 1, 4 * LANES), lambda bi, ci: (ci, 0, 0)),
            pl.BlockSpec((2, cb), lambda bi, ci: (0, ci)),
        ],
        out_specs=seq_spec,
        out_shape=jax.ShapeDtypeStruct((b, s, c), jnp.bfloat16),
        scratch_shapes=[pltpu.VMEM((s + 16, cb), jnp.float32)]
        + [pltpu.VMEM((s, cb), jnp.float32) for _ in range(4)],
        compiler_params=pltpu.CompilerParams(
            dimension_semantics=("arbitrary", "arbitrary"), vmem_limit_bytes=VMEM_LIMIT),
        name="rglru",
    )(xr, gr, conv_w, conv_b, wg, bg, lru_lambda)


def _lru_gate_weights(w_rg, b_rg, w_ig, b_ig):
    n_groups = D_LRU // LANES
    per = LANES // LRU_BLOCK

    def blockdiag(w):
        w = w.reshape(n_groups, per, LRU_BLOCK, LRU_BLOCK)
        eye = jnp.eye(per, dtype=w.dtype)
        return jnp.einsum('gpde,pq->gpdqe', w, eye).reshape(n_groups, LANES, LANES)

    wg = jnp.concatenate([blockdiag(w_rg[0]), blockdiag(w_ig[0]),
                          blockdiag(w_rg[1]), blockdiag(w_ig[1])], axis=-1)
    bg = jnp.concatenate([b.reshape(n_groups, 1, LANES) for b in (b_rg[0], b_ig[0], b_rg[1], b_ig[1])],
                         axis=-1)
    return wg.astype(jnp.bfloat16), bg


def _ffn_kernel(x_ref, ya_ref, yl_ref, wo_ref, g2_ref, wgate_ref, wup_ref, wdown_ref, gf_ref, o_ref,
                act_ref):
    h1 = x_ref[...] + _dot(ya_ref[...], wo_ref[:D_ATTN, :]) + _dot(yl_ref[...], wo_ref[D_ATTN:, :])
    ms = jnp.mean(h1 * h1, axis=-1, keepdims=True)
    n2 = (h1 * lax.rsqrt(ms + NORM_EPS) * g2_ref[...]).astype(jnp.bfloat16)
    for c in range(D_FF // FFN_CHUNK):
        cols = slice(c * FFN_CHUNK, (c + 1) * FFN_CHUNK)
        gate = _dot(n2, wgate_ref[:, cols])
        up = _dot(n2, wup_ref[:, cols])
        act_ref[:, cols] = (gate * jax.nn.sigmoid(gate) * up).astype(jnp.bfloat16)
    h2 = h1 + _dot(act_ref[...], wdown_ref[...])
    ms = jnp.mean(h2 * h2, axis=-1, keepdims=True)
    o_ref[...] = h2 * lax.rsqrt(ms + NORM_EPS) * gf_ref[...]


def _outproj_ffn(x2, ya, yl, wo, g2, wgate, wup, wdown, gf, tm=512):
    n_tok, d = x2.shape
    once = lambda shape: pl.BlockSpec(shape, lambda ti: (0,) * len(shape),
                                      pipeline_mode=pl.Buffered(1))
    return pl.pallas_call(
        _ffn_kernel,
        grid=(n_tok // tm,),
        in_specs=[
            pl.BlockSpec((tm, d), lambda ti: (ti, 0)),
            pl.BlockSpec((tm, D_ATTN), lambda ti: (ti, 0)),
            pl.BlockSpec((tm, D_LRU), lambda ti: (ti, 0)),
            once((d, d)), once((1, d)),
            once((d, D_FF)), once((d, D_FF)), once((D_FF, d)), once((1, d)),
        ],
        out_specs=pl.BlockSpec((tm, d), lambda ti: (ti, 0)),
        out_shape=jax.ShapeDtypeStruct((n_tok, d), jnp.float32),
        scratch_shapes=[pltpu.VMEM((tm, D_FF), jnp.bfloat16)],
        compiler_params=pltpu.CompilerParams(
            dimension_semantics=("arbitrary",), vmem_limit_bytes=VMEM_LIMIT),
        name="outproj_ffn",
    )(x2, ya, yl, wo, g2, wgate, wup, wdown, gf)


def kernel(x, attn_norm_g, w_in, lambda_q1, lambda_k1, lambda_q2, lambda_k2, subln_g, rel_bias,
           conv_w, conv_b, w_rg, b_rg, w_ig, b_ig, lru_lambda, w_out, ffn_norm_g, w_gate, w_up,
           w_down, final_norm_g):
    b, s, d = x.shape
    bf16 = jnp.bfloat16
    w = w_in[0]
    wqt = w[:, :D_ATTN].T.astype(bf16)
    wk = w[:, D_ATTN:2 * D_ATTN].astype(bf16)
    wvt = w[:, 2 * D_ATTN:3 * D_ATTN].T.astype(bf16)
    wx = w[:, 3 * D_ATTN:3 * D_ATTN + D_LRU].astype(bf16)
    wgr = w[:, 3 * D_ATTN + D_LRU:].astype(bf16)

    qt, k, vt, xr, gr = _inproj(x, attn_norm_g[0][None, :], wqt, wk, wvt, wx, wgr)

    bias = _bias_tiles(rel_bias)
    lamv = jnp.stack([lambda_q1[0], lambda_k1[0], lambda_q2[0], lambda_k2[0]])
    y_attn = _attention(rel_bias, lamv, subln_g[0][None, :], qt, k, vt, bias)

    wg, bg = _lru_gate_weights(w_rg[0], b_rg[0], w_ig[0], b_ig[0])
    y_lru = _rglru(xr, gr, conv_w[0], conv_b[0][None, :], wg, bg, lru_lambda[0])

    out = _outproj_ffn(
        x.reshape(b * s, d), y_attn.reshape(b * s, D_ATTN), y_lru.reshape(b * s, D_LRU),
        w_out[0].astype(bf16), ffn_norm_g[0][None, :], w_gate[0].astype(bf16),
        w_up[0].astype(bf16), w_down[0].astype(bf16), final_norm_g[None, :])
    return out.reshape(b, s, d)
```

```python
import functools
import math

import jax
import jax.numpy as jnp
from jax import lax
from jax.experimental import pallas as pl
from jax.experimental.pallas import tpu as pltpu

D_MODEL = 1024
D_ATTN = 512
D_LRU = 512
HEAD_DIM = 64
N_HEADS = 4
V_DIM = 128
NUM_BUCKETS = 32
LRU_BLOCK = 64
LRU_C = 8.0
CONV_WIDTH = 4
D_FF = 2816
NORM_EPS = 1e-6
LAMBDA_INIT = 0.2
LOG2E = 1.4426950408889634

LANES = 128
ATTN_TILE = 256
ATTN_SUBTILES = 2
ONES_ROWS = 16
LRU_CHANNELS = 256
LRU_CHUNK = 512
FFN_CHUNK = 256
VMEM_LIMIT = 56 * 1024 * 1024

_BUCKET_THRESHOLDS = (12, 16, 23, 32, 46, 64, 91)
assert _BUCKET_THRESHOLDS[-1] <= ATTN_TILE


def _nt_dot(a, b):
    return lax.dot_general(a, b, (((1,), (1,)), ((), ())), preferred_element_type=jnp.float32)


def _dot(a, b):
    return jnp.dot(a, b, preferred_element_type=jnp.float32)


def _bias_kernel(table_ref, out_ref):
    h = pl.program_id(0)
    d = pl.program_id(1)
    t = ATTN_TILE
    kk = lax.broadcasted_iota(jnp.int32, (t, t), 0)
    qq = lax.broadcasted_iota(jnp.int32, (t, t), 1)
    rel = (d - 1) * t + kk - qq
    n = jnp.abs(rel)
    large = jnp.full((t, t), NUM_BUCKETS // 4, jnp.int32)
    for thr in _BUCKET_THRESHOLDS:
        large = large + (n >= thr).astype(jnp.int32)
    bucket = jnp.where(rel > 0, NUM_BUCKETS // 2, 0) + jnp.where(n < NUM_BUCKETS // 4, n, large)
    acc = jnp.zeros((t, t), jnp.float32)
    for b in range(NUM_BUCKETS):
        acc = jnp.where(bucket == b, table_ref[b, h], acc)
    out_ref[0, 0] = acc * LOG2E


def _bias_tiles(rel_bias):
    t = ATTN_TILE
    return pl.pallas_call(
        _bias_kernel,
        grid=(N_HEADS, 3),
        in_specs=[pl.BlockSpec(memory_space=pltpu.SMEM)],
        out_specs=pl.BlockSpec((1, 1, t, t), lambda h, d: (h, d, 0, 0)),
        out_shape=jax.ShapeDtypeStruct((N_HEADS, 3, t, t), jnp.float32),
        name="bias_tiles",
    )(rel_bias)


def _inproj_kernel(x_ref, g_ref, wqt_ref, wk_ref, wvt_ref, wx_ref, wg_ref,
                   qt_ref, k_ref, vt_ref, xr_ref, gr_ref):
    x = x_ref[0]
    ms = jnp.mean(x * x, axis=-1, keepdims=True)
    n = (x * lax.rsqrt(ms + NORM_EPS) * g_ref[...]).astype(jnp.bfloat16)
    qt = _nt_dot(wqt_ref[...], n)
    qt_ref[0] = (qt * (LOG2E / math.sqrt(HEAD_DIM))).astype(jnp.bfloat16)
    k_ref[0] = _dot(n, wk_ref[...]).astype(jnp.bfloat16)
    vt = _nt_dot(wvt_ref[...], n).astype(jnp.bfloat16)
    for c in range(vt_ref.shape[1]):
        vt_ref[0, c] = vt[:, c * ATTN_TILE:(c + 1) * ATTN_TILE]
    xr_ref[0] = _dot(n, wx_ref[...])
    gr_ref[0] = _dot(n, wg_ref[...])


def _inproj(x, g, wqt, wk, wvt, wx, wg, tm=512):
    b, s, d = x.shape
    nc = s // ATTN_TILE
    cpt = tm // ATTN_TILE
    const = lambda shape: pl.BlockSpec(shape, lambda bi, ti: (0,) * len(shape))
    return pl.pallas_call(
        _inproj_kernel,
        grid=(b, s // tm),
        in_specs=[
            pl.BlockSpec((1, tm, d), lambda bi, ti: (bi, ti, 0)),
            const((1, d)),
            const((D_ATTN, d)), const((d, D_ATTN)), const((D_ATTN, d)),
            const((d, D_LRU)), const((d, D_LRU)),
        ],
        out_specs=[
            pl.BlockSpec((1, D_ATTN, tm), lambda bi, ti: (bi, 0, ti)),
            pl.BlockSpec((1, tm, D_ATTN), lambda bi, ti: (bi, ti, 0)),
            pl.BlockSpec((1, cpt, D_ATTN, ATTN_TILE), lambda bi, ti: (bi, ti, 0, 0)),
            pl.BlockSpec((1, tm, D_LRU), lambda bi, ti: (bi, ti, 0)),
            pl.BlockSpec((1, tm, D_LRU), lambda bi, ti: (bi, ti, 0)),
        ],
        out_shape=[
            jax.ShapeDtypeStruct((b, D_ATTN, s), jnp.bfloat16),
            jax.ShapeDtypeStruct((b, s, D_ATTN), jnp.bfloat16),
            jax.ShapeDtypeStruct((b, nc, D_ATTN, ATTN_TILE), jnp.bfloat16),
            jax.ShapeDtypeStruct((b, s, D_LRU), jnp.float32),
            jax.ShapeDtypeStruct((b, s, D_LRU), jnp.float32),
        ],
        compiler_params=pltpu.CompilerParams(
            dimension_semantics=("arbitrary", "arbitrary"), vmem_limit_bytes=VMEM_LIMIT),
        name="inproj",
    )(x, g, wqt, wk, wvt, wx, wg)


def _attn_kernel(table_ref, lamv_ref, g_ref, qt_ref, k_ref, vt_ref, bias_ref, o_ref,
                 qz_ref, s_ref, m_ref, acc_ref, *, n_chunks):
    t = ATTN_TILE
    ip = pl.program_id(1)
    n_pairs = n_chunks // 2
    units = [(h, u) for h in range(N_HEADS) for u in range(ATTN_SUBTILES)]
    n_units = len(units)

    z = jnp.zeros((HEAD_DIM, t), jnp.bfloat16)
    for x, (h, u) in enumerate(units):
        qt = qt_ref[0, h * V_DIM:(h + 1) * V_DIM, u * t:(u + 1) * t]
        qz_ref[x] = jnp.concatenate([jnp.concatenate([qt[:HEAD_DIM], z], axis=1),
                                     jnp.concatenate([z, qt[HEAD_DIM:]], axis=1)], axis=0)
    ones = jnp.ones((ONES_ROWS, t), jnp.bfloat16)
    m_ref[...] = jnp.full(m_ref.shape, -1e30, jnp.float32)
    acc_ref[...] = jnp.zeros(acc_ref.shape, jnp.float32)

    c_left = [table_ref[NUM_BUCKETS // 2 - 1, h] * LOG2E for h in range(N_HEADS)]
    c_right = [table_ref[NUM_BUCKETS - 1, h] * LOG2E for h in range(N_HEADS)]

    def scores(j, slot, x):
        h, _ = units[x]
        rows = pl.ds(pl.multiple_of(j * t, t), t)
        s_ref[slot, x] = _dot(k_ref[0, rows, h * V_DIM:(h + 1) * V_DIM], qz_ref[x])

    def accumulate(j, slot, x, kind):
        h, _ = units[x]
        s = s_ref[slot, x]
        const = None
        if kind == 'left':
            const = c_left[h]
        elif kind == 'right':
            const = c_right[h]
        else:
            bias = bias_ref[h, kind]
            s = s + jnp.concatenate([bias, bias], axis=1)
        m_old = m_ref[x]
        m_cur = jnp.max(s, axis=0, keepdims=True)
        if const is not None:
            m_cur = m_cur + const
        m_new = jnp.maximum(m_old, m_cur)
        alpha = jnp.exp2(m_old - m_new)
        shift = m_new if const is None else m_new - const
        p = jnp.exp2(s - shift).astype(jnp.bfloat16)
        vta = jnp.concatenate([vt_ref[0, j, h * V_DIM:(h + 1) * V_DIM, :], ones], axis=0)
        acc_ref[x] = acc_ref[x] * alpha + _dot(vta, p)
        m_ref[x] = m_new

    def pair_body(a, kinds):
        for c in range(2):
            j = 2 * a + c
            j_next = jnp.minimum(j + 1, n_chunks - 1)
            lead = 2
            for x in range(lead):
                scores(j_next, 1 - c, x)
            for x in range(n_units):
                if x + lead < n_units:
                    scores(j_next, 1 - c, x + lead)
                accumulate(j, c, x, kinds[c][units[x][1]])

    far_left = (('left', 'left'), ('left', 'left'))
    far_right = (('right', 'right'), ('right', 'right'))
    before = (('left', 'left'), (0, 'left'))
    diagonal = ((1, 0), (2, 1))
    after = (('right', 2), ('right', 'right'))

    for x in range(n_units):
        scores(0, 0, x)

    def left_loop(a, carry):
        pair_body(a, far_left)
        return carry

    def right_loop(a, carry):
        pair_body(a, far_right)
        return carry

    lax.fori_loop(0, jnp.maximum(ip - 1, 0), left_loop, 0)

    @pl.when(ip >= 1)
    def _():
        pair_body(ip - 1, before)

    pair_body(ip, diagonal)

    @pl.when(ip + 1 < n_pairs)
    def _():
        pair_body(ip + 1, after)

    lax.fori_loop(jnp.minimum(ip + 2, n_pairs), n_pairs, right_loop, 0)

    lv = lamv_ref[...]
    lam = (jnp.exp(jnp.sum(lv[0:1] * lv[1:2], axis=-1, keepdims=True))
           - jnp.exp(jnp.sum(lv[2:3] * lv[3:4], axis=-1, keepdims=True)) + LAMBDA_INIT)
    for x, (h, u) in enumerate(units):
        acc = acc_ref[x]
        o1 = acc[:V_DIM, :t] / acc[V_DIM:V_DIM + 1, :t]
        o2 = acc[:V_DIM, t:] / acc[V_DIM:V_DIM + 1, t:]
        o = o1 - lam * o2
        ms = jnp.mean(o * o, axis=0, keepdims=True)
        y = (o * lax.rsqrt(ms + NORM_EPS)).T
        o_ref[0, u * t:(u + 1) * t, h * V_DIM:(h + 1) * V_DIM] = (
            y * g_ref[...] * (1.0 - LAMBDA_INIT)).astype(o_ref.dtype)


def _attention(rel_bias, lamv, subln_g, qt, k, vt, bias):
    b, _, s = qt.shape
    t = ATTN_TILE
    tq = ATTN_SUBTILES * t
    n_chunks = s // t
    n_units = N_HEADS * ATTN_SUBTILES
    smem = pl.BlockSpec(memory_space=pltpu.SMEM)
    return pl.pallas_call(
        functools.partial(_attn_kernel, n_chunks=n_chunks),
        grid=(b, s // tq),
        in_specs=[
            smem,
            pl.BlockSpec((4, HEAD_DIM), lambda bi, qi: (0, 0)),
            pl.BlockSpec((1, V_DIM), lambda bi, qi: (0, 0)),
            pl.BlockSpec((1, D_ATTN, tq), lambda bi, qi: (bi, 0, qi)),
            pl.BlockSpec((1, s, D_ATTN), lambda bi, qi: (bi, 0, 0)),
            pl.BlockSpec((1, n_chunks, D_ATTN, t), lambda bi, qi: (bi, 0, 0, 0)),
            pl.BlockSpec((N_HEADS, 3, t, t), lambda bi, qi: (0, 0, 0, 0)),
        ],
        out_specs=pl.BlockSpec((1, tq, D_ATTN), lambda bi, qi: (bi, qi, 0)),
        out_shape=jax.ShapeDtypeStruct((b, s, D_ATTN), jnp.bfloat16),
        scratch_shapes=[pltpu.VMEM((n_units, V_DIM, 2 * t), jnp.bfloat16),
                        pltpu.VMEM((2, n_units, t, 2 * t), jnp.float32),
                        pltpu.VMEM((n_units, 1, 2 * t), jnp.float32),
                        pltpu.VMEM((n_units, V_DIM + ONES_ROWS, 2 * t), jnp.float32)],
        compiler_params=pltpu.CompilerParams(
            dimension_semantics=("arbitrary", "arbitrary"),
            vmem_limit_bytes=VMEM_LIMIT),
        name="diff_attention",
    )(rel_bias, lamv, subln_g, qt, k, vt, bias)


def _lru_kernel(xr_ref, gr_ref, cw_ref, cb_ref, wg_ref, bg_ref, lam_ref, o_ref,
                xp_ref, af_ref, uf_ref, ab_ref, ub_ref, *, seq):
    cb = LRU_CHANNELS
    pad = 8
    xp_ref[0:pad, :] = jnp.zeros((pad, cb), jnp.float32)
    xp_ref[pad + seq:2 * pad + seq, :] = jnp.zeros((pad, cb), jnp.float32)
    xp_ref[pad:pad + seq, :] = xr_ref[0]

    cw = cw_ref[...]
    cbias = cb_ref[...]
    sp = jnp.log1p(jnp.exp(-lam_ref[...]))
    n = LRU_CHUNK

    def gate_chunk(ci, carry):
        t0 = pl.multiple_of(ci * n, n)
        blk = xp_ref[pl.ds(t0, n + 2 * pad), :]
        xc = cbias
        for tap in range(CONV_WIDTH):
            lo = pad - CONV_WIDTH // 2 + tap
            xc = xc + blk[lo:lo + n] * cw[tap:tap + 1]
        for g in range(cb // LANES):
            cols = slice(g * LANES, (g + 1) * LANES)
            xg = xc[:, cols]
            gates = _dot(xg.astype(jnp.bfloat16), wg_ref[g]) + bg_ref[g]
            for direction, (a_ref, u_ref) in enumerate(((af_ref, uf_ref), (ab_ref, ub_ref))):
                r = jax.nn.sigmoid(gates[:, (2 * direction) * LANES:(2 * direction + 1) * LANES])
                ig = jax.nn.sigmoid(gates[:, (2 * direction + 1) * LANES:(2 * direction + 2) * LANES])
                log_a = (-LRU_C) * r * sp[direction:direction + 1, cols]
                a_ref[pl.ds(t0, n), cols] = jnp.exp(log_a)
                u_ref[pl.ds(t0, n), cols] = jnp.sqrt(1.0 - jnp.exp(2.0 * log_a)) * (ig * xg)
        return carry

    lax.fori_loop(0, seq // n, gate_chunk, 0)

    def scan_body(t, carry):
        hf, hb = carry
        tb = seq - 1 - t
        hf = af_ref[pl.ds(t, 1), :] * hf + uf_ref[pl.ds(t, 1), :]
        uf_ref[pl.ds(t, 1), :] = hf
        hb = ab_ref[pl.ds(tb, 1), :] * hb + ub_ref[pl.ds(tb, 1), :]
        ub_ref[pl.ds(tb, 1), :] = hb
        return hf, hb

    zero = jnp.zeros((1, cb), jnp.float32)
    lax.fori_loop(0, seq, scan_body, (zero, zero), unroll=8)

    def out_chunk(ci, carry):
        t0 = pl.multiple_of(ci * n, n)
        rows = pl.ds(t0, n)
        gate = jax.nn.gelu(gr_ref[0, rows, :])
        o_ref[0, rows, :] = (gate * (uf_ref[rows, :] + ub_ref[rows, :])).astype(o_ref.dtype)
        return carry

    lax.fori_loop(0, seq // n, out_chunk, 0)


def _rglru(xr, gr, conv_w, conv_b, wg, bg, lru_lambda):
    b, s, c = xr.shape
    cb = LRU_CHANNELS
    groups = cb // LANES
    seq_spec = pl.BlockSpec((1, s, cb), lambda bi, ci: (bi, 0, ci))
    return pl.pallas_call(
        functools.partial(_lru_kernel, seq=s),
        grid=(b, c // cb),
        in_specs=[
            seq_spec, seq_spec,
            pl.BlockSpec((CONV_WIDTH, cb), lambda bi, ci: (0, ci)),
            pl.BlockSpec((1, cb), lambda bi, ci: (0, ci)),
            pl.BlockSpec((groups, LANES, 4 * LANES), lambda bi, ci: (ci, 0, 0)),
            pl.BlockSpec((groups, 1, 4 * LANES), lambda bi, ci: (ci, 0, 0)),
            pl.BlockSpec((2, cb), lambda bi, ci: (0, ci)),
        ],
        out_specs=seq_spec,
        out_shape=jax.ShapeDtypeStruct((b, s, c), jnp.bfloat16),
        scratch_shapes=[pltpu.VMEM((s + 16, cb), jnp.float32)]
        + [pltpu.VMEM((s, cb), jnp.float32) for _ in range(4)],
        compiler_params=pltpu.CompilerParams(
            dimension_semantics=("arbitrary", "arbitrary"), vmem_limit_bytes=VMEM_LIMIT),
        name="rglru",
    )(xr, gr, conv_w, conv_b, wg, bg, lru_lambda)


def _lru_gate_weights(w_rg, b_rg, w_ig, b_ig):
    n_groups = D_LRU // LANES
    per = LANES // LRU_BLOCK

    def blockdiag(w):
        w = w.reshape(n_groups, per, LRU_BLOCK, LRU_BLOCK)
        eye = jnp.eye(per, dtype=w.dtype)
        return jnp.einsum('gpde,pq->gpdqe', w, eye).reshape(n_groups, LANES, LANES)

    wg = jnp.concatenate([blockdiag(w_rg[0]), blockdiag(w_ig[0]),
                          blockdiag(w_rg[1]), blockdiag(w_ig[1])], axis=-1)
    bg = jnp.concatenate([b.reshape(n_groups, 1, LANES) for b in (b_rg[0], b_ig[0], b_rg[1], b_ig[1])],
                         axis=-1)
    return wg.astype(jnp.bfloat16), bg


def _ffn_kernel(x_ref, ya_ref, yl_ref, wo_ref, g2_ref, wgate_ref, wup_ref, wdown_ref, gf_ref, o_ref,
                act_ref):
    h1 = x_ref[...] + _dot(ya_ref[...], wo_ref[:D_ATTN, :]) + _dot(yl_ref[...], wo_ref[D_ATTN:, :])
    ms = jnp.mean(h1 * h1, axis=-1, keepdims=True)
    n2 = (h1 * lax.rsqrt(ms + NORM_EPS) * g2_ref[...]).astype(jnp.bfloat16)
    for c in range(D_FF // FFN_CHUNK):
        cols = slice(c * FFN_CHUNK, (c + 1) * FFN_CHUNK)
        gate = _dot(n2, wgate_ref[:, cols])
        up = _dot(n2, wup_ref[:, cols])
        act_ref[:, cols] = (gate * jax.nn.sigmoid(gate) * up).astype(jnp.bfloat16)
    h2 = h1 + _dot(act_ref[...], wdown_ref[...])
    ms = jnp.mean(h2 * h2, axis=-1, keepdims=True)
    o_ref[...] = h2 * lax.rsqrt(ms + NORM_EPS) * gf_ref[...]


def _outproj_ffn(x2, ya, yl, wo, g2, wgate, wup, wdown, gf, tm=512):
    n_tok, d = x2.shape
    once = lambda shape: pl.BlockSpec(shape, lambda ti: (0,) * len(shape),
                                      pipeline_mode=pl.Buffered(1))
    return pl.pallas_call(
        _ffn_kernel,
        grid=(n_tok // tm,),
        in_specs=[
            pl.BlockSpec((tm, d), lambda ti: (ti, 0)),
            pl.BlockSpec((tm, D_ATTN), lambda ti: (ti, 0)),
            pl.BlockSpec((tm, D_LRU), lambda ti: (ti, 0)),
            once((d, d)), once((1, d)),
            once((d, D_FF)), once((d, D_FF)), once((D_FF, d)), once((1, d)),
        ],
        out_specs=pl.BlockSpec((tm, d), lambda ti: (ti, 0)),
        out_shape=jax.ShapeDtypeStruct((n_tok, d), jnp.float32),
        scratch_shapes=[pltpu.VMEM((tm, D_FF), jnp.bfloat16)],
        compiler_params=pltpu.CompilerParams(
            dimension_semantics=("arbitrary",), vmem_limit_bytes=VMEM_LIMIT),
        name="outproj_ffn",
    )(x2, ya, yl, wo, g2, wgate, wup, wdown, gf)


def kernel(x, attn_norm_g, w_in, lambda_q1, lambda_k1, lambda_q2, lambda_k2, subln_g, rel_bias,
           conv_w, conv_b, w_rg, b_rg, w_ig, b_ig, lru_lambda, w_out, ffn_norm_g, w_gate, w_up,
           w_down, final_norm_g):
    b, s, d = x.shape
    bf16 = jnp.bfloat16
    w = w_in[0]
    wqt = w[:, :D_ATTN].T.astype(bf16)
    wk = w[:, D_ATTN:2 * D_ATTN].astype(bf16)
    wvt = w[:, 2 * D_ATTN:3 * D_ATTN].T.astype(bf16)
    wx = w[:, 3 * D_ATTN:3 * D_ATTN + D_LRU].astype(bf16)
    wgr = w[:, 3 * D_ATTN + D_LRU:].astype(bf16)

    qt, k, vt, xr, gr = _inproj(x, attn_norm_g[0][None, :], wqt, wk, wvt, wx, wgr)

    bias = _bias_tiles(rel_bias)
    lamv = jnp.stack([lambda_q1[0], lambda_k1[0], lambda_q2[0], lambda_k2[0]])
    y_attn = _attention(rel_bias, lamv, subln_g[0][None, :], qt, k, vt, bias)

    wg, bg = _lru_gate_weights(w_rg[0], b_rg[0], w_ig[0], b_ig[0])
    y_lru = _rglru(xr, gr, conv_w[0], conv_b[0][None, :], wg, bg, lru_lambda[0])

    out = _outproj_ffn(
        x.reshape(b * s, d), y_attn.reshape(b * s, D_ATTN), y_lru.reshape(b * s, D_LRU),
        w_out[0].astype(bf16), ffn_norm_g[0][None, :], w_gate[0].astype(bf16),
        w_up[0].astype(bf16), w_down[0].astype(bf16), final_norm_g[None, :])
    return out.reshape(b, s, d)
```

```python
import functools
import math

import jax
import jax.numpy as jnp
from jax import lax
from jax.experimental import pallas as pl
from jax.experimental.pallas import tpu as pltpu

D_MODEL = 1024
D_ATTN = 512
D_LRU = 512
HEAD_DIM = 64
N_HEADS = 4
V_DIM = 128
NUM_BUCKETS = 32
LRU_BLOCK = 64
LRU_C = 8.0
CONV_WIDTH = 4
D_FF = 2816
NORM_EPS = 1e-6
LAMBDA_INIT = 0.2
LOG2E = 1.4426950408889634

LANES = 128
ATTN_TILE = 256
ATTN_SUBTILES = 4
ATTN_LEAD = 3
ONES_ROWS = 16
LRU_CHANNELS = 256
LRU_CHUNK = 512
FFN_CHUNK = 256
VMEM_LIMIT = 56 * 1024 * 1024

_BUCKET_THRESHOLDS = (12, 16, 23, 32, 46, 64, 91)
assert _BUCKET_THRESHOLDS[-1] <= ATTN_TILE


def _nt_dot(a, b):
    return lax.dot_general(a, b, (((1,), (1,)), ((), ())), preferred_element_type=jnp.float32)


def _dot(a, b):
    return jnp.dot(a, b, preferred_element_type=jnp.float32)


def _bias_kernel(table_ref, out_ref):
    h = pl.program_id(0)
    d = pl.program_id(1)
    t = ATTN_TILE
    kk = lax.broadcasted_iota(jnp.int32, (t, t), 0)
    qq = lax.broadcasted_iota(jnp.int32, (t, t), 1)
    rel = (d - 1) * t + kk - qq
    n = jnp.abs(rel)
    large = jnp.full((t, t), NUM_BUCKETS // 4, jnp.int32)
    for thr in _BUCKET_THRESHOLDS:
        large = large + (n >= thr).astype(jnp.int32)
    bucket = jnp.where(rel > 0, NUM_BUCKETS // 2, 0) + jnp.where(n < NUM_BUCKETS // 4, n, large)
    acc = jnp.zeros((t, t), jnp.float32)
    for b in range(NUM_BUCKETS):
        acc = jnp.where(bucket == b, table_ref[b, h], acc)
    out_ref[0, 0] = acc * LOG2E


def _bias_tiles(rel_bias):
    t = ATTN_TILE
    return pl.pallas_call(
        _bias_kernel,
        grid=(N_HEADS, 3),
        in_specs=[pl.BlockSpec(memory_space=pltpu.SMEM)],
        out_specs=pl.BlockSpec((1, 1, t, t), lambda h, d: (h, d, 0, 0)),
        out_shape=jax.ShapeDtypeStruct((N_HEADS, 3, t, t), jnp.float32),
        name="bias_tiles",
    )(rel_bias)


def _inproj_kernel(x_ref, g_ref, wqt_ref, wk_ref, wvt_ref, wx_ref, wg_ref,
                   qt_ref, k_ref, vt_ref, xr_ref, gr_ref):
    x = x_ref[0]
    ms = jnp.mean(x * x, axis=-1, keepdims=True)
    n = (x * lax.rsqrt(ms + NORM_EPS) * g_ref[...]).astype(jnp.bfloat16)
    qt = _nt_dot(wqt_ref[...], n)
    qt_ref[0] = (qt * (LOG2E / math.sqrt(HEAD_DIM))).astype(jnp.bfloat16)
    k_ref[0] = _dot(n, wk_ref[...]).astype(jnp.bfloat16)
    vt = _nt_dot(wvt_ref[...], n).astype(jnp.bfloat16)
    for c in range(vt_ref.shape[1]):
        vt_ref[0, c] = vt[:, c * ATTN_TILE:(c + 1) * ATTN_TILE]
    xr_ref[0] = _dot(n, wx_ref[...])
    gr_ref[0] = _dot(n, wg_ref[...])


def _inproj(x, g, wqt, wk, wvt, wx, wg, tm=512):
    b, s, d = x.shape
    nc = s // ATTN_TILE
    cpt = tm // ATTN_TILE
    const = lambda shape: pl.BlockSpec(shape, lambda bi, ti: (0,) * len(shape))
    return pl.pallas_call(
        _inproj_kernel,
        grid=(b, s // tm),
        in_specs=[
            pl.BlockSpec((1, tm, d), lambda bi, ti: (bi, ti, 0)),
            const((1, d)),
            const((D_ATTN, d)), const((d, D_ATTN)), const((D_ATTN, d)),
            const((d, D_LRU)), const((d, D_LRU)),
        ],
        out_specs=[
            pl.BlockSpec((1, D_ATTN, tm), lambda bi, ti: (bi, 0, ti)),
            pl.BlockSpec((1, tm, D_ATTN), lambda bi, ti: (bi, ti, 0)),
            pl.BlockSpec((1, cpt, D_ATTN, ATTN_TILE), lambda bi, ti: (bi, ti, 0, 0)),
            pl.BlockSpec((1, tm, D_LRU), lambda bi, ti: (bi, ti, 0)),
            pl.BlockSpec((1, tm, D_LRU), lambda bi, ti: (bi, ti, 0)),
        ],
        out_shape=[
            jax.ShapeDtypeStruct((b, D_ATTN, s), jnp.bfloat16),
            jax.ShapeDtypeStruct((b, s, D_ATTN), jnp.bfloat16),
            jax.ShapeDtypeStruct((b, nc, D_ATTN, ATTN_TILE), jnp.bfloat16),
            jax.ShapeDtypeStruct((b, s, D_LRU), jnp.float32),
            jax.ShapeDtypeStruct((b, s, D_LRU), jnp.float32),
        ],
        compiler_params=pltpu.CompilerParams(
            dimension_semantics=("arbitrary", "arbitrary"), vmem_limit_bytes=VMEM_LIMIT),
        name="inproj",
    )(x, g, wqt, wk, wvt, wx, wg)


def _attn_kernel(table_ref, lamv_ref, g_ref, qt_ref, k_ref, vt_ref, bias_ref, o_ref,
                 qz_ref, s_ref, m_ref, acc_ref, *, n_chunks):
    t = ATTN_TILE
    ip = pl.program_id(1)
    n_pairs = n_chunks // 2
    units = [(h, u) for h in range(N_HEADS) for u in range(ATTN_SUBTILES)]
    n_units = len(units)

    z = jnp.zeros((HEAD_DIM, t), jnp.bfloat16)
    for x, (h, u) in enumerate(units):
        qt = qt_ref[0, h * V_DIM:(h + 1) * V_DIM, u * t:(u + 1) * t]
        qz_ref[x] = jnp.concatenate([jnp.concatenate([qt[:HEAD_DIM], z], axis=1),
                                     jnp.concatenate([z, qt[HEAD_DIM:]], axis=1)], axis=0)
    ones = jnp.ones((ONES_ROWS, t), jnp.bfloat16)
    m_ref[...] = jnp.full(m_ref.shape, -1e30, jnp.float32)
    acc_ref[...] = jnp.zeros(acc_ref.shape, jnp.float32)

    c_left = [table_ref[NUM_BUCKETS // 2 - 1, h] * LOG2E for h in range(N_HEADS)]
    c_right = [table_ref[NUM_BUCKETS - 1, h] * LOG2E for h in range(N_HEADS)]

    def scores(j, slot, x):
        h, _ = units[x]
        rows = pl.ds(pl.multiple_of(j * t, t), t)
        s_ref[slot, x] = _dot(k_ref[0, rows, h * V_DIM:(h + 1) * V_DIM], qz_ref[x])

    def accumulate(j, slot, x, kind):
        h, _ = units[x]
        s = s_ref[slot, x]
        const = None
        if kind == 'left':
            const = c_left[h]
        elif kind == 'right':
            const = c_right[h]
        else:
            bias = bias_ref[h, kind]
            s = s + jnp.concatenate([bias, bias], axis=1)
        m_old = m_ref[x]
        m_cur = jnp.max(s, axis=0, keepdims=True)
        if const is not None:
            m_cur = m_cur + const
        m_new = jnp.maximum(m_old, m_cur)
        alpha = jnp.exp2(m_old - m_new)
        shift = m_new if const is None else m_new - const
        p = jnp.exp2(s - shift).astype(jnp.bfloat16)
        vta = jnp.concatenate([vt_ref[0, j, h * V_DIM:(h + 1) * V_DIM, :], ones], axis=0)
        acc_ref[x] = acc_ref[x] * alpha + _dot(vta, p)
        m_ref[x] = m_new

    def pair_kinds(pa):
        def kind(d):
            return d + 1 if abs(d) <= 1 else ('left' if d < 0 else 'right')
        return [[kind(2 * pa + c - u) for u in range(ATTN_SUBTILES)] for c in range(2)]

    def pair_body(a, kinds):
        for c in range(2):
            j = 2 * a + c
            j_next = jnp.minimum(j + 1, n_chunks - 1)
            for x in range(ATTN_LEAD):
                scores(j_next, 1 - c, x)
            for x in range(n_units):
                if x + ATTN_LEAD < n_units:
                    scores(j_next, 1 - c, x + ATTN_LEAD)
                accumulate(j, c, x, kinds[c][units[x][1]])

    for x in range(n_units):
        scores(0, 0, x)

    def left_loop(a, carry):
        pair_body(a, pair_kinds(-2))
        return carry

    def right_loop(a, carry):
        pair_body(a, pair_kinds(ATTN_SUBTILES))
        return carry

    half = ATTN_SUBTILES // 2
    first = half * ip
    lax.fori_loop(0, jnp.maximum(first - 1, 0), left_loop, 0)
    for pa in range(-1, half + 1):
        @pl.when(jnp.logical_and(first + pa >= 0, first + pa < n_pairs))
        def _():
            pair_body(first + pa, pair_kinds(pa))
    lax.fori_loop(jnp.minimum(first + half + 1, n_pairs), n_pairs, right_loop, 0)

    lv = lamv_ref[...]
    lam = (jnp.exp(jnp.sum(lv[0:1] * lv[1:2], axis=-1, keepdims=True))
           - jnp.exp(jnp.sum(lv[2:3] * lv[3:4], axis=-1, keepdims=True)) + LAMBDA_INIT)
    for x, (h, u) in enumerate(units):
        acc = acc_ref[x]
        o1 = acc[:V_DIM, :t] / acc[V_DIM:V_DIM + 1, :t]
        o2 = acc[:V_DIM, t:] / acc[V_DIM:V_DIM + 1, t:]
        o = o1 - lam * o2
        ms = jnp.mean(o * o, axis=0, keepdims=True)
        y = (o * lax.rsqrt(ms + NORM_EPS)).T
        o_ref[0, u * t:(u + 1) * t, h * V_DIM:(h + 1) * V_DIM] = (
            y * g_ref[...] * (1.0 - LAMBDA_INIT)).astype(o_ref.dtype)


def _attention(rel_bias, lamv, subln_g, qt, k, vt, bias):
    b, _, s = qt.shape
    t = ATTN_TILE
    tq = ATTN_SUBTILES * t
    n_chunks = s // t
    n_units = N_HEADS * ATTN_SUBTILES
    smem = pl.BlockSpec(memory_space=pltpu.SMEM)
    return pl.pallas_call(
        functools.partial(_attn_kernel, n_chunks=n_chunks),
        grid=(b, s // tq),
        in_specs=[
            smem,
            pl.BlockSpec((4, HEAD_DIM), lambda bi, qi: (0, 0)),
            pl.BlockSpec((1, V_DIM), lambda bi, qi: (0, 0)),
            pl.BlockSpec((1, D_ATTN, tq), lambda bi, qi: (bi, 0, qi)),
            pl.BlockSpec((1, s, D_ATTN), lambda bi, qi: (bi, 0, 0)),
            pl.BlockSpec((1, n_chunks, D_ATTN, t), lambda bi, qi: (bi, 0, 0, 0)),
            pl.BlockSpec((N_HEADS, 3, t, t), lambda bi, qi: (0, 0, 0, 0)),
        ],
        out_specs=pl.BlockSpec((1, tq, D_ATTN), lambda bi, qi: (bi, qi, 0)),
        out_shape=jax.ShapeDtypeStruct((b, s, D_ATTN), jnp.bfloat16),
        scratch_shapes=[pltpu.VMEM((n_units, V_DIM, 2 * t), jnp.bfloat16),
                        pltpu.VMEM((2, n_units, t, 2 * t), jnp.float32),
                        pltpu.VMEM((n_units, 1, 2 * t), jnp.float32),
                        pltpu.VMEM((n_units, V_DIM + ONES_ROWS, 2 * t), jnp.float32)],
        compiler_params=pltpu.CompilerParams(
            dimension_semantics=("arbitrary", "arbitrary"),
            vmem_limit_bytes=VMEM_LIMIT),
        name="diff_attention",
    )(rel_bias, lamv, subln_g, qt, k, vt, bias)


def _lru_kernel(xr_ref, gr_ref, cw_ref, cb_ref, wg_ref, bg_ref, lam_ref, o_ref,
                xp_ref, af_ref, uf_ref, ab_ref, ub_ref, *, seq):
    cb = LRU_CHANNELS
    pad = 8
    xp_ref[0:pad, :] = jnp.zeros((pad, cb), jnp.float32)
    xp_ref[pad + seq:2 * pad + seq, :] = jnp.zeros((pad, cb), jnp.float32)
    xp_ref[pad:pad + seq, :] = xr_ref[0]

    cw = cw_ref[...]
    cbias = cb_ref[...]
    sp = jnp.log1p(jnp.exp(-lam_ref[...]))
    n = LRU_CHUNK

    def gate_chunk(ci, carry):
        t0 = pl.multiple_of(ci * n, n)
        blk = xp_ref[pl.ds(t0, n + 2 * pad), :]
        xc = cbias
        for tap in range(CONV_WIDTH):
            lo = pad - CONV_WIDTH // 2 + tap
            xc = xc + blk[lo:lo + n] * cw[tap:tap + 1]
        for g in range(cb // LANES):
            cols = slice(g * LANES, (g + 1) * LANES)
            xg = xc[:, cols]
            gates = _dot(xg.astype(jnp.bfloat16), wg_ref[g]) + bg_ref[g]
            for direction, (a_ref, u_ref) in enumerate(((af_ref, uf_ref), (ab_ref, ub_ref))):
                r = jax.nn.sigmoid(gates[:, (2 * direction) * LANES:(2 * direction + 1) * LANES])
                ig = jax.nn.sigmoid(gates[:, (2 * direction + 1) * LANES:(2 * direction + 2) * LANES])
                log_a = (-LRU_C) * r * sp[direction:direction + 1, cols]
                a_ref[pl.ds(t0, n), cols] = jnp.exp(log_a)
                u_ref[pl.ds(t0, n), cols] = jnp.sqrt(1.0 - jnp.exp(2.0 * log_a)) * (ig * xg)
        return carry

    lax.fori_loop(0, seq // n, gate_chunk, 0)

    row = lax.broadcasted_iota(jnp.int32, (pad, cb), 0)

    def tile_scan(a, u, reverse):
        for d in (1, 2, 4):
            if reverse:
                keep = row < pad - d
                shift = pad - d
            else:
                keep = row >= d
                shift = d
            a_prev = jnp.where(keep, pltpu.roll(a, shift, axis=0), 1.0)
            u_prev = jnp.where(keep, pltpu.roll(u, shift, axis=0), 0.0)
            u = u + a * u_prev
            a = a * a_prev
        return a, u

    def scan_body(k, carry):
        hf, hb = carry
        rows_f = pl.ds(pl.multiple_of(k * pad, pad), pad)
        rows_b = pl.ds(pl.multiple_of(seq - pad - k * pad, pad), pad)
        a, u = tile_scan(af_ref[rows_f, :], uf_ref[rows_f, :], reverse=False)
        uf_ref[rows_f, :] = u + a * hf
        hf = u[pad - 1:pad] + a[pad - 1:pad] * hf
        a, u = tile_scan(ab_ref[rows_b, :], ub_ref[rows_b, :], reverse=True)
        ub_ref[rows_b, :] = u + a * hb
        hb = u[0:1] + a[0:1] * hb
        return hf, hb

    zero = jnp.zeros((1, cb), jnp.float32)
    lax.fori_loop(0, seq // pad, scan_body, (zero, zero), unroll=4)

    def out_chunk(ci, carry):
        t0 = pl.multiple_of(ci * n, n)
        rows = pl.ds(t0, n)
        gate = jax.nn.gelu(gr_ref[0, rows, :])
        o_ref[0, rows, :] = (gate * (uf_ref[rows, :] + ub_ref[rows, :])).astype(o_ref.dtype)
        return carry

    lax.fori_loop(0, seq // n, out_chunk, 0)


def _rglru(xr, gr, conv_w, conv_b, wg, bg, lru_lambda):
    b, s, c = xr.shape
    cb = LRU_CHANNELS
    groups = cb // LANES
    seq_spec = pl.BlockSpec((1, s, cb), lambda bi, ci: (bi, 0, ci))
    return pl.pallas_call(
        functools.partial(_lru_kernel, seq=s),
        grid=(b, c // cb),
        in_specs=[
            seq_spec, seq_spec,
            pl.BlockSpec((CONV_WIDTH, cb), lambda bi, ci: (0, ci)),
            pl.BlockSpec((1, cb), lambda bi, ci: (0, ci)),
            pl.BlockSpec((groups, LANES, 4 * LANES), lambda bi, ci: (ci, 0, 0)),
            pl.BlockSpec((groups, 1, 4 * LANES), lambda bi, ci: (ci, 0, 0)),
            pl.BlockSpec((2, cb), lambda bi, ci: (0, ci)),
        ],
        out_specs=seq_spec,
        out_shape=jax.ShapeDtypeStruct((b, s, c), jnp.bfloat16),
        scratch_shapes=[pltpu.VMEM((s + 16, cb), jnp.float32)]
        + [pltpu.VMEM((s, cb), jnp.float32) for _ in range(4)],
        compiler_params=pltpu.CompilerParams(
            dimension_semantics=("arbitrary", "arbitrary"), vmem_limit_bytes=VMEM_LIMIT),
        name="rglru",
    )(xr, gr, conv_w, conv_b, wg, bg, lru_lambda)


def _lru_gate_weights(w_rg, b_rg, w_ig, b_ig):
    n_groups = D_LRU // LANES
    per = LANES // LRU_BLOCK

    def blockdiag(w):
        w = w.reshape(n_groups, per, LRU_BLOCK, LRU_BLOCK)
        eye = jnp.eye(per, dtype=w.dtype)
        return jnp.einsum('gpde,pq->gpdqe', w, eye).reshape(n_groups, LANES, LANES)

    wg = jnp.concatenate([blockdiag(w_rg[0]), blockdiag(w_ig[0]),
                          blockdiag(w_rg[1]), blockdiag(w_ig[1])], axis=-1)
    bg = jnp.concatenate([b.reshape(n_groups, 1, LANES) for b in (b_rg[0], b_ig[0], b_rg[1], b_ig[1])],
                         axis=-1)
    return wg.astype(jnp.bfloat16), bg


def _ffn_kernel(x_ref, ya_ref, yl_ref, wo_ref, g2_ref, wgate_ref, wup_ref, wdown_ref, gf_ref, o_ref,
                act_ref):
    h1 = x_ref[...] + _dot(ya_ref[...], wo_ref[:D_ATTN, :]) + _dot(yl_ref[...], wo_ref[D_ATTN:, :])
    ms = jnp.mean(h1 * h1, axis=-1, keepdims=True)
    n2 = (h1 * lax.rsqrt(ms + NORM_EPS) * g2_ref[...]).astype(jnp.bfloat16)
    for c in range(D_FF // FFN_CHUNK):
        cols = slice(c * FFN_CHUNK, (c + 1) * FFN_CHUNK)
        gate = _dot(n2, wgate_ref[:, cols])
        up = _dot(n2, wup_ref[:, cols])
        act_ref[:, cols] = (gate * jax.nn.sigmoid(gate) * up).astype(jnp.bfloat16)
    h2 = h1 + _dot(act_ref[...], wdown_ref[...])
    ms = jnp.mean(h2 * h2, axis=-1, keepdims=True)
    o_ref[...] = h2 * lax.rsqrt(ms + NORM_EPS) * gf_ref[...]


def _outproj_ffn(x2, ya, yl, wo, g2, wgate, wup, wdown, gf, tm=512):
    n_tok, d = x2.shape
    once = lambda shape: pl.BlockSpec(shape, lambda ti: (0,) * len(shape),
                                      pipeline_mode=pl.Buffered(1))
    return pl.pallas_call(
        _ffn_kernel,
        grid=(n_tok // tm,),
        in_specs=[
            pl.BlockSpec((tm, d), lambda ti: (ti, 0)),
            pl.BlockSpec((tm, D_ATTN), lambda ti: (ti, 0)),
            pl.BlockSpec((tm, D_LRU), lambda ti: (ti, 0)),
            once((d, d)), once((1, d)),
            once((d, D_FF)), once((d, D_FF)), once((D_FF, d)), once((1, d)),
        ],
        out_specs=pl.BlockSpec((tm, d), lambda ti: (ti, 0)),
        out_shape=jax.ShapeDtypeStruct((n_tok, d), jnp.float32),
        scratch_shapes=[pltpu.VMEM((tm, D_FF), jnp.bfloat16)],
        compiler_params=pltpu.CompilerParams(
            dimension_semantics=("arbitrary",), vmem_limit_bytes=VMEM_LIMIT),
        name="outproj_ffn",
    )(x2, ya, yl, wo, g2, wgate, wup, wdown, gf)


def kernel(x, attn_norm_g, w_in, lambda_q1, lambda_k1, lambda_q2, lambda_k2, subln_g, rel_bias,
           conv_w, conv_b, w_rg, b_rg, w_ig, b_ig, lru_lambda, w_out, ffn_norm_g, w_gate, w_up,
           w_down, final_norm_g):
    b, s, d = x.shape
    bf16 = jnp.bfloat16
    w = w_in[0]
    wqt = w[:, :D_ATTN].T.astype(bf16)
    wk = w[:, D_ATTN:2 * D_ATTN].astype(bf16)
    wvt = w[:, 2 * D_ATTN:3 * D_ATTN].T.astype(bf16)
    wx = w[:, 3 * D_ATTN:3 * D_ATTN + D_LRU].astype(bf16)
    wgr = w[:, 3 * D_ATTN + D_LRU:].astype(bf16)

    qt, k, vt, xr, gr = _inproj(x, attn_norm_g[0][None, :], wqt, wk, wvt, wx, wgr)

    bias = _bias_tiles(rel_bias)
    lamv = jnp.stack([lambda_q1[0], lambda_k1[0], lambda_q2[0], lambda_k2[0]])
    y_attn = _attention(rel_bias, lamv, subln_g[0][None, :], qt, k, vt, bias)

    wg, bg = _lru_gate_weights(w_rg[0], b_rg[0], w_ig[0], b_ig[0])
    y_lru = _rglru(xr, gr, conv_w[0], conv_b[0][None, :], wg, bg, lru_lambda[0])

    out = _outproj_ffn(
        x.reshape(b * s, d), y_attn.reshape(b * s, D_ATTN), y_lru.reshape(b * s, D_LRU),
        w_out[0].astype(bf16), ffn_norm_g[0][None, :], w_gate[0].astype(bf16),
        w_up[0].astype(bf16), w_down[0].astype(bf16), final_norm_g[None, :])
    return out.reshape(b, s, d)
```

```python
import functools
import math

import jax
import jax.numpy as jnp
from jax import lax
from jax.experimental import pallas as pl
from jax.experimental.pallas import tpu as pltpu

D_MODEL = 1024
D_ATTN = 512
D_LRU = 512
HEAD_DIM = 64
N_HEADS = 4
V_DIM = 128
NUM_BUCKETS = 32
LRU_BLOCK = 64
LRU_C = 8.0
CONV_WIDTH = 4
D_FF = 2816
NORM_EPS = 1e-6
LAMBDA_INIT = 0.2
LOG2E = 1.4426950408889634

LANES = 128
ATTN_TILE = 256
ATTN_SUBTILES = 4
ATTN_LEAD = 3
ONES_ROWS = 16
LRU_CHANNELS = 256
LRU_CHUNK = 512
LRU_UNROLL = 8
FFN_CHUNK = 256
VMEM_LIMIT = 56 * 1024 * 1024

_BUCKET_THRESHOLDS = (12, 16, 23, 32, 46, 64, 91)
assert _BUCKET_THRESHOLDS[-1] <= ATTN_TILE


def _nt_dot(a, b):
    return lax.dot_general(a, b, (((1,), (1,)), ((), ())), preferred_element_type=jnp.float32)


def _dot(a, b):
    return jnp.dot(a, b, preferred_element_type=jnp.float32)


def _bias_kernel(table_ref, out_ref):
    h = pl.program_id(0)
    d = pl.program_id(1)
    t = ATTN_TILE
    kk = lax.broadcasted_iota(jnp.int32, (t, t), 0)
    qq = lax.broadcasted_iota(jnp.int32, (t, t), 1)
    rel = (d - 1) * t + kk - qq
    n = jnp.abs(rel)
    large = jnp.full((t, t), NUM_BUCKETS // 4, jnp.int32)
    for thr in _BUCKET_THRESHOLDS:
        large = large + (n >= thr).astype(jnp.int32)
    bucket = jnp.where(rel > 0, NUM_BUCKETS // 2, 0) + jnp.where(n < NUM_BUCKETS // 4, n, large)
    acc = jnp.zeros((t, t), jnp.float32)
    for b in range(NUM_BUCKETS):
        acc = jnp.where(bucket == b, table_ref[b, h], acc)
    out_ref[0, 0] = acc * LOG2E


def _bias_tiles(rel_bias):
    t = ATTN_TILE
    return pl.pallas_call(
        _bias_kernel,
        grid=(N_HEADS, 3),
        in_specs=[pl.BlockSpec(memory_space=pltpu.SMEM)],
        out_specs=pl.BlockSpec((1, 1, t, t), lambda h, d: (h, d, 0, 0)),
        out_shape=jax.ShapeDtypeStruct((N_HEADS, 3, t, t), jnp.float32),
        name="bias_tiles",
    )(rel_bias)


def _inproj_kernel(x_ref, g_ref, wqt_ref, wk_ref, wvt_ref, wx_ref, wg_ref,
                   qt_ref, k_ref, vt_ref, xr_ref, gr_ref):
    x = x_ref[0]
    ms = jnp.mean(x * x, axis=-1, keepdims=True)
    n = (x * lax.rsqrt(ms + NORM_EPS) * g_ref[...]).astype(jnp.bfloat16)
    qt = _nt_dot(wqt_ref[...], n)
    qt_ref[0] = (qt * (LOG2E / math.sqrt(HEAD_DIM))).astype(jnp.bfloat16)
    k_ref[0] = _dot(n, wk_ref[...]).astype(jnp.bfloat16)
    vt = _nt_dot(wvt_ref[...], n).astype(jnp.bfloat16)
    for c in range(vt_ref.shape[1]):
        vt_ref[0, c] = vt[:, c * ATTN_TILE:(c + 1) * ATTN_TILE]
    xr_ref[0] = _dot(n, wx_ref[...])
    gr_ref[0] = _dot(n, wg_ref[...])


def _inproj(x, g, wqt, wk, wvt, wx, wg, tm=512):
    b, s, d = x.shape
    nc = s // ATTN_TILE
    cpt = tm // ATTN_TILE
    const = lambda shape: pl.BlockSpec(shape, lambda bi, ti: (0,) * len(shape))
    return pl.pallas_call(
        _inproj_kernel,
        grid=(b, s // tm),
        in_specs=[
            pl.BlockSpec((1, tm, d), lambda bi, ti: (bi, ti, 0)),
            const((1, d)),
            const((D_ATTN, d)), const((d, D_ATTN)), const((D_ATTN, d)),
            const((d, D_LRU)), const((d, D_LRU)),
        ],
        out_specs=[
            pl.BlockSpec((1, D_ATTN, tm), lambda bi, ti: (bi, 0, ti)),
            pl.BlockSpec((1, tm, D_ATTN), lambda bi, ti: (bi, ti, 0)),
            pl.BlockSpec((1, cpt, D_ATTN, ATTN_TILE), lambda bi, ti: (bi, ti, 0, 0)),
            pl.BlockSpec((1, tm, D_LRU), lambda bi, ti: (bi, ti, 0)),
            pl.BlockSpec((1, tm, D_LRU), lambda bi, ti: (bi, ti, 0)),
        ],
        out_shape=[
            jax.ShapeDtypeStruct((b, D_ATTN, s), jnp.bfloat16),
            jax.ShapeDtypeStruct((b, s, D_ATTN), jnp.bfloat16),
            jax.ShapeDtypeStruct((b, nc, D_ATTN, ATTN_TILE), jnp.bfloat16),
            jax.ShapeDtypeStruct((b, s, D_LRU), jnp.float32),
            jax.ShapeDtypeStruct((b, s, D_LRU), jnp.float32),
        ],
        compiler_params=pltpu.CompilerParams(
            dimension_semantics=("arbitrary", "arbitrary"), vmem_limit_bytes=VMEM_LIMIT),
        name="inproj",
    )(x, g, wqt, wk, wvt, wx, wg)


def _attn_kernel(table_ref, lamv_ref, g_ref, qt_ref, k_ref, vt_ref, bias_ref, o_ref,
                 qz_ref, s_ref, m_ref, acc_ref, *, n_chunks):
    t = ATTN_TILE
    ip = pl.program_id(1)
    n_pairs = n_chunks // 2
    units = [(h, u) for h in range(N_HEADS) for u in range(ATTN_SUBTILES)]
    n_units = len(units)

    z = jnp.zeros((HEAD_DIM, t), jnp.bfloat16)
    for x, (h, u) in enumerate(units):
        qt = qt_ref[0, h * V_DIM:(h + 1) * V_DIM, u * t:(u + 1) * t]
        qz_ref[x] = jnp.concatenate([jnp.concatenate([qt[:HEAD_DIM], z], axis=1),
                                     jnp.concatenate([z, qt[HEAD_DIM:]], axis=1)], axis=0)
    ones = jnp.ones((ONES_ROWS, t), jnp.bfloat16)
    m_ref[...] = jnp.full(m_ref.shape, -1e30, jnp.float32)
    acc_ref[...] = jnp.zeros(acc_ref.shape, jnp.float32)

    c_left = [table_ref[NUM_BUCKETS // 2 - 1, h] * LOG2E for h in range(N_HEADS)]
    c_right = [table_ref[NUM_BUCKETS - 1, h] * LOG2E for h in range(N_HEADS)]

    def scores(j, slot, x):
        h, _ = units[x]
        rows = pl.ds(pl.multiple_of(j * t, t), t)
        s_ref[slot, x] = _dot(k_ref[0, rows, h * V_DIM:(h + 1) * V_DIM], qz_ref[x])

    def accumulate(j, slot, x, kind):
        h, _ = units[x]
        s = s_ref[slot, x]
        const = None
        if kind == 'left':
            const = c_left[h]
        elif kind == 'right':
            const = c_right[h]
        else:
            bias = bias_ref[h, kind]
            s = s + jnp.concatenate([bias, bias], axis=1)
        m_old = m_ref[x]
        m_cur = jnp.max(s, axis=0, keepdims=True)
        if const is not None:
            m_cur = m_cur + const
        m_new = jnp.maximum(m_old, m_cur)
        alpha = jnp.exp2(m_old - m_new)
        shift = m_new if const is None else m_new - const
        p = jnp.exp2(s - shift).astype(jnp.bfloat16)
        vta = jnp.concatenate([vt_ref[0, j, h * V_DIM:(h + 1) * V_DIM, :], ones], axis=0)
        acc_ref[x] = acc_ref[x] * alpha + _dot(vta, p)
        m_ref[x] = m_new

    def pair_kinds(pa):
        def kind(d):
            return d + 1 if abs(d) <= 1 else ('left' if d < 0 else 'right')
        return [[kind(2 * pa + c - u) for u in range(ATTN_SUBTILES)] for c in range(2)]

    def pair_body(a, kinds):
        for c in range(2):
            j = 2 * a + c
            j_next = jnp.minimum(j + 1, n_chunks - 1)
            for x in range(ATTN_LEAD):
                scores(j_next, 1 - c, x)
            for x in range(n_units):
                if x + ATTN_LEAD < n_units:
                    scores(j_next, 1 - c, x + ATTN_LEAD)
                accumulate(j, c, x, kinds[c][units[x][1]])

    for x in range(n_units):
        scores(0, 0, x)

    def left_loop(a, carry):
        pair_body(a, pair_kinds(-2))
        return carry

    def right_loop(a, carry):
        pair_body(a, pair_kinds(ATTN_SUBTILES))
        return carry

    half = ATTN_SUBTILES // 2
    first = half * ip
    lax.fori_loop(0, jnp.maximum(first - 1, 0), left_loop, 0)
    for pa in range(-1, half + 1):
        @pl.when(jnp.logical_and(first + pa >= 0, first + pa < n_pairs))
        def _():
            pair_body(first + pa, pair_kinds(pa))
    lax.fori_loop(jnp.minimum(first + half + 1, n_pairs), n_pairs, right_loop, 0)

    lv = lamv_ref[...]
    lam = (jnp.exp(jnp.sum(lv[0:1] * lv[1:2], axis=-1, keepdims=True))
           - jnp.exp(jnp.sum(lv[2:3] * lv[3:4], axis=-1, keepdims=True)) + LAMBDA_INIT)
    for x, (h, u) in enumerate(units):
        acc = acc_ref[x]
        o1 = acc[:V_DIM, :t] / acc[V_DIM:V_DIM + 1, :t]
        o2 = acc[:V_DIM, t:] / acc[V_DIM:V_DIM + 1, t:]
        o = o1 - lam * o2
        ms = jnp.mean(o * o, axis=0, keepdims=True)
        y = (o * lax.rsqrt(ms + NORM_EPS)).T
        o_ref[0, u * t:(u + 1) * t, h * V_DIM:(h + 1) * V_DIM] = (
            y * g_ref[...] * (1.0 - LAMBDA_INIT)).astype(o_ref.dtype)


def _attention(rel_bias, lamv, subln_g, qt, k, vt, bias):
    b, _, s = qt.shape
    t = ATTN_TILE
    tq = ATTN_SUBTILES * t
    n_chunks = s // t
    n_units = N_HEADS * ATTN_SUBTILES
    smem = pl.BlockSpec(memory_space=pltpu.SMEM)
    return pl.pallas_call(
        functools.partial(_attn_kernel, n_chunks=n_chunks),
        grid=(b, s // tq),
        in_specs=[
            smem,
            pl.BlockSpec((4, HEAD_DIM), lambda bi, qi: (0, 0)),
            pl.BlockSpec((1, V_DIM), lambda bi, qi: (0, 0)),
            pl.BlockSpec((1, D_ATTN, tq), lambda bi, qi: (bi, 0, qi)),
            pl.BlockSpec((1, s, D_ATTN), lambda bi, qi: (bi, 0, 0)),
            pl.BlockSpec((1, n_chunks, D_ATTN, t), lambda bi, qi: (bi, 0, 0, 0)),
            pl.BlockSpec((N_HEADS, 3, t, t), lambda bi, qi: (0, 0, 0, 0)),
        ],
        out_specs=pl.BlockSpec((1, tq, D_ATTN), lambda bi, qi: (bi, qi, 0)),
        out_shape=jax.ShapeDtypeStruct((b, s, D_ATTN), jnp.bfloat16),
        scratch_shapes=[pltpu.VMEM((n_units, V_DIM, 2 * t), jnp.bfloat16),
                        pltpu.VMEM((2, n_units, t, 2 * t), jnp.float32),
                        pltpu.VMEM((n_units, 1, 2 * t), jnp.float32),
                        pltpu.VMEM((n_units, V_DIM + ONES_ROWS, 2 * t), jnp.float32)],
        compiler_params=pltpu.CompilerParams(
            dimension_semantics=("arbitrary", "arbitrary"),
            vmem_limit_bytes=VMEM_LIMIT),
        name="diff_attention",
    )(rel_bias, lamv, subln_g, qt, k, vt, bias)


def _lru_kernel(xr_ref, gr_ref, cw_ref, cb_ref, wg_ref, bg_ref, lam_ref, o_ref,
                xp_ref, af_ref, uf_ref, ab_ref, ub_ref, hfo_ref, hbo_ref, *, seq):
    cb = LRU_CHANNELS
    pad = 8
    xp_ref[0:pad, :] = jnp.zeros((pad, cb), jnp.float32)
    xp_ref[pad + seq:2 * pad + seq, :] = jnp.zeros((pad, cb), jnp.float32)
    xp_ref[pad:pad + seq, :] = xr_ref[0]

    cw = cw_ref[...]
    cbias = cb_ref[...]
    decay = jnp.log1p(jnp.exp(-lam_ref[...])) * (-LRU_C * LOG2E)
    n = LRU_CHUNK
    n_seg = seq // n
    groups = cb // LANES

    def seg_rows(ci):
        return pl.ds(ci, n, stride=n_seg)

    def gate_chunk(ci, carry):
        t0 = pl.multiple_of(ci * n, n)
        blk = xp_ref[pl.ds(t0, n + 2 * pad), :]
        xc = cbias
        for tap in range(CONV_WIDTH):
            lo = pad - CONV_WIDTH // 2 + tap
            xc = xc + blk[lo:lo + n] * cw[tap:tap + 1]
        for g in range(groups):
            cols = slice(g * LANES, (g + 1) * LANES)
            xg = xc[:, cols]
            gates = _dot(xg.astype(jnp.bfloat16), wg_ref[g]) + bg_ref[g]
            for direction, (a_ref, u_ref) in enumerate(((af_ref, uf_ref), (ab_ref, ub_ref))):
                r = jax.nn.sigmoid(gates[:, (2 * direction) * LANES:(2 * direction + 1) * LANES])
                ig = jax.nn.sigmoid(gates[:, (2 * direction + 1) * LANES:(2 * direction + 2) * LANES])
                a = jnp.exp2(r * decay[direction:direction + 1, cols])
                a_ref[g, seg_rows(ci), :] = a
                u_ref[g, seg_rows(ci), :] = jnp.sqrt(1.0 - a * a) * (ig * xg)
        return carry

    lax.fori_loop(0, n_seg, gate_chunk, 0)

    def tiles(k):
        rows_f = pl.ds(pl.multiple_of(k * n_seg, n_seg), n_seg)
        rows_b = pl.ds(pl.multiple_of((n - 1 - k) * n_seg, n_seg), n_seg)
        return rows_f, rows_b

    def ends_body(k, carry):
        rows_f, rows_b = tiles(k)
        out = []
        for g in range(groups):
            hf, pf, hb, pb = carry[g]
            a = af_ref[g, rows_f, :]
            hf = a * hf + uf_ref[g, rows_f, :]
            pf = a * pf
            a = ab_ref[g, rows_b, :]
            hb = a * hb + ub_ref[g, rows_b, :]
            pb = a * pb
            out.append((hf, pf, hb, pb))
        return tuple(out)

    zero = jnp.zeros((n_seg, LANES), jnp.float32)
    one = jnp.ones((n_seg, LANES), jnp.float32)
    ends = lax.fori_loop(0, n, ends_body, tuple((zero, one, zero, one) for _ in range(groups)),
                         unroll=LRU_UNROLL)

    starts = []
    for g in range(groups):
        hf, pf, hb, pb = ends[g]
        state = jnp.zeros((1, LANES), jnp.float32)
        fwd = [state]
        for r in range(1, n_seg):
            state = hf[r - 1:r] + pf[r - 1:r] * state
            fwd.append(state)
        state = jnp.zeros((1, LANES), jnp.float32)
        bwd = [state]
        for r in range(n_seg - 2, -1, -1):
            state = hb[r + 1:r + 2] + pb[r + 1:r + 2] * state
            bwd.insert(0, state)
        starts.append((jnp.concatenate(fwd, axis=0), jnp.concatenate(bwd, axis=0)))

    def scan_body(k, carry):
        rows_f, rows_b = tiles(k)
        out = []
        for g in range(groups):
            hf, hb = carry[g]
            hf = af_ref[g, rows_f, :] * hf + uf_ref[g, rows_f, :]
            hfo_ref[g, rows_f, :] = hf
            hb = ab_ref[g, rows_b, :] * hb + ub_ref[g, rows_b, :]
            hbo_ref[g, rows_b, :] = hb
            out.append((hf, hb))
        return tuple(out)

    lax.fori_loop(0, n, scan_body, tuple(starts), unroll=LRU_UNROLL)

    def out_chunk(ci, carry):
        t0 = pl.multiple_of(ci * n, n)
        for g in range(groups):
            cols = slice(g * LANES, (g + 1) * LANES)
            gate = jax.nn.gelu(gr_ref[0, pl.ds(t0, n), cols])
            o_ref[0, pl.ds(t0, n), cols] = (
                gate * (hfo_ref[g, seg_rows(ci), :] + hbo_ref[g, seg_rows(ci), :])).astype(o_ref.dtype)
        return carry

    lax.fori_loop(0, n_seg, out_chunk, 0)


def _rglru(xr, gr, conv_w, conv_b, wg, bg, lru_lambda):
    b, s, c = xr.shape
    cb = LRU_CHANNELS
    groups = cb // LANES
    seq_spec = pl.BlockSpec((1, s, cb), lambda bi, ci: (bi, 0, ci))
    return pl.pallas_call(
        functools.partial(_lru_kernel, seq=s),
        grid=(b, c // cb),
        in_specs=[
            seq_spec, seq_spec,
            pl.BlockSpec((CONV_WIDTH, cb), lambda bi, ci: (0, ci)),
            pl.BlockSpec((1, cb), lambda bi, ci: (0, ci)),
            pl.BlockSpec((groups, LANES, 4 * LANES), lambda bi, ci: (ci, 0, 0)),
            pl.BlockSpec((groups, 1, 4 * LANES), lambda bi, ci: (ci, 0, 0)),
            pl.BlockSpec((2, cb), lambda bi, ci: (0, ci)),
        ],
        out_specs=seq_spec,
        out_shape=jax.ShapeDtypeStruct((b, s, c), jnp.bfloat16),
        scratch_shapes=[pltpu.VMEM((s + 16, cb), jnp.float32)]
        + [pltpu.VMEM((groups, s, LANES), jnp.float32) for _ in range(6)],
        compiler_params=pltpu.CompilerParams(
            dimension_semantics=("arbitrary", "arbitrary"), vmem_limit_bytes=VMEM_LIMIT),
        name="rglru",
    )(xr, gr, conv_w, conv_b, wg, bg, lru_lambda)


def _lru_gate_weights(w_rg, b_rg, w_ig, b_ig):
    n_groups = D_LRU // LANES
    per = LANES // LRU_BLOCK

    def blockdiag(w):
        w = w.reshape(n_groups, per, LRU_BLOCK, LRU_BLOCK)
        eye = jnp.eye(per, dtype=w.dtype)
        return jnp.einsum('gpde,pq->gpdqe', w, eye).reshape(n_groups, LANES, LANES)

    wg = jnp.concatenate([blockdiag(w_rg[0]), blockdiag(w_ig[0]),
                          blockdiag(w_rg[1]), blockdiag(w_ig[1])], axis=-1)
    bg = jnp.concatenate([b.reshape(n_groups, 1, LANES) for b in (b_rg[0], b_ig[0], b_rg[1], b_ig[1])],
                         axis=-1)
    return wg.astype(jnp.bfloat16), bg


def _ffn_kernel(x_ref, ya_ref, yl_ref, wo_ref, g2_ref, wgate_ref, wup_ref, wdown_ref, gf_ref, o_ref,
                act_ref):
    h1 = x_ref[...] + _dot(ya_ref[...], wo_ref[:D_ATTN, :]) + _dot(yl_ref[...], wo_ref[D_ATTN:, :])
    ms = jnp.mean(h1 * h1, axis=-1, keepdims=True)
    n2 = (h1 * lax.rsqrt(ms + NORM_EPS) * g2_ref[...]).astype(jnp.bfloat16)
    for c in range(D_FF // FFN_CHUNK):
        cols = slice(c * FFN_CHUNK, (c + 1) * FFN_CHUNK)
        gate = _dot(n2, wgate_ref[:, cols])
        up = _dot(n2, wup_ref[:, cols])
        act_ref[:, cols] = (gate * jax.nn.sigmoid(gate) * up).astype(jnp.bfloat16)
    h2 = h1 + _dot(act_ref[...], wdown_ref[...])
    ms = jnp.mean(h2 * h2, axis=-1, keepdims=True)
    o_ref[...] = h2 * lax.rsqrt(ms + NORM_EPS) * gf_ref[...]


def _outproj_ffn(x2, ya, yl, wo, g2, wgate, wup, wdown, gf, tm=512):
    n_tok, d = x2.shape
    once = lambda shape: pl.BlockSpec(shape, lambda ti: (0,) * len(shape),
                                      pipeline_mode=pl.Buffered(1))
    return pl.pallas_call(
        _ffn_kernel,
        grid=(n_tok // tm,),
        in_specs=[
            pl.BlockSpec((tm, d), lambda ti: (ti, 0)),
            pl.BlockSpec((tm, D_ATTN), lambda ti: (ti, 0)),
            pl.BlockSpec((tm, D_LRU), lambda ti: (ti, 0)),
            once((d, d)), once((1, d)),
            once((d, D_FF)), once((d, D_FF)), once((D_FF, d)), once((1, d)),
        ],
        out_specs=pl.BlockSpec((tm, d), lambda ti: (ti, 0)),
        out_shape=jax.ShapeDtypeStruct((n_tok, d), jnp.float32),
        scratch_shapes=[pltpu.VMEM((tm, D_FF), jnp.bfloat16)],
        compiler_params=pltpu.CompilerParams(
            dimension_semantics=("arbitrary",), vmem_limit_bytes=VMEM_LIMIT),
        name="outproj_ffn",
    )(x2, ya, yl, wo, g2, wgate, wup, wdown, gf)


def kernel(x, attn_norm_g, w_in, lambda_q1, lambda_k1, lambda_q2, lambda_k2, subln_g, rel_bias,
           conv_w, conv_b, w_rg, b_rg, w_ig, b_ig, lru_lambda, w_out, ffn_norm_g, w_gate, w_up,
           w_down, final_norm_g):
    b, s, d = x.shape
    bf16 = jnp.bfloat16
    w = w_in[0]
    wqt = w[:, :D_ATTN].T.astype(bf16)
    wk = w[:, D_ATTN:2 * D_ATTN].astype(bf16)
    wvt = w[:, 2 * D_ATTN:3 * D_ATTN].T.astype(bf16)
    wx = w[:, 3 * D_ATTN:3 * D_ATTN + D_LRU].astype(bf16)
    wgr = w[:, 3 * D_ATTN + D_LRU:].astype(bf16)

    qt, k, vt, xr, gr = _inproj(x, attn_norm_g[0][None, :], wqt, wk, wvt, wx, wgr)

    bias = _bias_tiles(rel_bias)
    lamv = jnp.stack([lambda_q1[0], lambda_k1[0], lambda_q2[0], lambda_k2[0]])
    y_attn = _attention(rel_bias, lamv, subln_g[0][None, :], qt, k, vt, bias)

    wg, bg = _lru_gate_weights(w_rg[0], b_rg[0], w_ig[0], b_ig[0])
    y_lru = _rglru(xr, gr, conv_w[0], conv_b[0][None, :], wg, bg, lru_lambda[0])

    out = _outproj_ffn(
        x.reshape(b * s, d), y_attn.reshape(b * s, D_ATTN), y_lru.reshape(b * s, D_LRU),
        w_out[0].astype(bf16), ffn_norm_g[0][None, :], w_gate[0].astype(bf16),
        w_up[0].astype(bf16), w_down[0].astype(bf16), final_norm_g[None, :])
    return out.reshape(b, s, d)
```

```python
import functools
import math

import jax
import jax.numpy as jnp
from jax import lax
from jax.experimental import pallas as pl
from jax.experimental.pallas import tpu as pltpu

D_MODEL = 1024
D_ATTN = 512
D_LRU = 512
HEAD_DIM = 64
N_HEADS = 4
V_DIM = 128
NUM_BUCKETS = 32
LRU_BLOCK = 64
LRU_C = 8.0
CONV_WIDTH = 4
D_FF = 2816
NORM_EPS = 1e-6
LAMBDA_INIT = 0.2
LOG2E = 1.4426950408889634

LANES = 128
ATTN_TILE = 256
ATTN_SUBTILES = 4
ATTN_LEAD = 3
ONES_ROWS = 16
LRU_CHANNELS = 256
LRU_CHUNK = 512
LRU_UNROLL = 8
FFN_CHUNK = 256
VMEM_LIMIT = 56 * 1024 * 1024

_BUCKET_THRESHOLDS = (12, 16, 23, 32, 46, 64, 91)
assert _BUCKET_THRESHOLDS[-1] <= ATTN_TILE


def _nt_dot(a, b):
    return lax.dot_general(a, b, (((1,), (1,)), ((), ())), preferred_element_type=jnp.float32)


def _dot(a, b):
    return jnp.dot(a, b, preferred_element_type=jnp.float32)


def _bias_kernel(table_ref, out_ref):
    h = pl.program_id(0)
    d = pl.program_id(1)
    t = ATTN_TILE
    kk = lax.broadcasted_iota(jnp.int32, (t, t), 0)
    qq = lax.broadcasted_iota(jnp.int32, (t, t), 1)
    rel = (d - 1) * t + kk - qq
    n = jnp.abs(rel)
    large = jnp.full((t, t), NUM_BUCKETS // 4, jnp.int32)
    for thr in _BUCKET_THRESHOLDS:
        large = large + (n >= thr).astype(jnp.int32)
    bucket = jnp.where(rel > 0, NUM_BUCKETS // 2, 0) + jnp.where(n < NUM_BUCKETS // 4, n, large)
    acc = jnp.zeros((t, t), jnp.float32)
    for b in range(NUM_BUCKETS):
        acc = jnp.where(bucket == b, table_ref[b, h], acc)
    out_ref[0, 0] = acc * LOG2E


def _bias_tiles(rel_bias):
    t = ATTN_TILE
    return pl.pallas_call(
        _bias_kernel,
        grid=(N_HEADS, 3),
        in_specs=[pl.BlockSpec(memory_space=pltpu.SMEM)],
        out_specs=pl.BlockSpec((1, 1, t, t), lambda h, d: (h, d, 0, 0)),
        out_shape=jax.ShapeDtypeStruct((N_HEADS, 3, t, t), jnp.float32),
        name="bias_tiles",
    )(rel_bias)


def _inproj_kernel(x_ref, g_ref, wqt_ref, wk_ref, wvt_ref, wx_ref, wg_ref,
                   qt_ref, k_ref, vt_ref, xr_ref, gr_ref):
    x = x_ref[0]
    ms = jnp.mean(x * x, axis=-1, keepdims=True)
    n = (x * lax.rsqrt(ms + NORM_EPS) * g_ref[...]).astype(jnp.bfloat16)
    qt = _nt_dot(wqt_ref[...], n)
    qt_ref[0] = (qt * (LOG2E / math.sqrt(HEAD_DIM))).astype(jnp.bfloat16)
    k_ref[0] = _dot(n, wk_ref[...]).astype(jnp.bfloat16)
    vt = _nt_dot(wvt_ref[...], n).astype(jnp.bfloat16)
    for c in range(vt_ref.shape[1]):
        vt_ref[0, c] = vt[:, c * ATTN_TILE:(c + 1) * ATTN_TILE]
    xr_ref[0] = _dot(n, wx_ref[...])
    gr_ref[0] = _dot(n, wg_ref[...])


def _inproj(x, g, wqt, wk, wvt, wx, wg, tm=512):
    b, s, d = x.shape
    nc = s // ATTN_TILE
    cpt = tm // ATTN_TILE
    const = lambda shape: pl.BlockSpec(shape, lambda bi, ti: (0,) * len(shape))
    return pl.pallas_call(
        _inproj_kernel,
        grid=(b, s // tm),
        in_specs=[
            pl.BlockSpec((1, tm, d), lambda bi, ti: (bi, ti, 0)),
            const((1, d)),
            const((D_ATTN, d)), const((d, D_ATTN)), const((D_ATTN, d)),
            const((d, D_LRU)), const((d, D_LRU)),
        ],
        out_specs=[
            pl.BlockSpec((1, D_ATTN, tm), lambda bi, ti: (bi, 0, ti)),
            pl.BlockSpec((1, tm, D_ATTN), lambda bi, ti: (bi, ti, 0)),
            pl.BlockSpec((1, cpt, D_ATTN, ATTN_TILE), lambda bi, ti: (bi, ti, 0, 0)),
            pl.BlockSpec((1, tm, D_LRU), lambda bi, ti: (bi, ti, 0)),
            pl.BlockSpec((1, tm, D_LRU), lambda bi, ti: (bi, ti, 0)),
        ],
        out_shape=[
            jax.ShapeDtypeStruct((b, D_ATTN, s), jnp.bfloat16),
            jax.ShapeDtypeStruct((b, s, D_ATTN), jnp.bfloat16),
            jax.ShapeDtypeStruct((b, nc, D_ATTN, ATTN_TILE), jnp.bfloat16),
            jax.ShapeDtypeStruct((b, s, D_LRU), jnp.float32),
            jax.ShapeDtypeStruct((b, s, D_LRU), jnp.float32),
        ],
        compiler_params=pltpu.CompilerParams(
            dimension_semantics=("arbitrary", "arbitrary"), vmem_limit_bytes=VMEM_LIMIT),
        name="inproj",
    )(x, g, wqt, wk, wvt, wx, wg)


def _attn_kernel(table_ref, lamv_ref, g_ref, qt_ref, k_ref, vt_ref, bias_ref, o_ref,
                 qz_ref, s_ref, m_ref, acc_ref, *, n_chunks):
    t = ATTN_TILE
    ip = pl.program_id(1)
    n_pairs = n_chunks // 2
    units = [(h, u) for h in range(N_HEADS) for u in range(ATTN_SUBTILES)]
    n_units = len(units)

    z = jnp.zeros((HEAD_DIM, t), jnp.bfloat16)
    for x, (h, u) in enumerate(units):
        qt = qt_ref[0, h * V_DIM:(h + 1) * V_DIM, u * t:(u + 1) * t]
        qz_ref[x] = jnp.concatenate([jnp.concatenate([qt[:HEAD_DIM], z], axis=1),
                                     jnp.concatenate([z, qt[HEAD_DIM:]], axis=1)], axis=0)
    ones = jnp.ones((ONES_ROWS, t), jnp.bfloat16)
    m_ref[...] = jnp.full(m_ref.shape, -1e30, jnp.float32)
    acc_ref[...] = jnp.zeros(acc_ref.shape, jnp.float32)

    c_left = [table_ref[NUM_BUCKETS // 2 - 1, h] * LOG2E for h in range(N_HEADS)]
    c_right = [table_ref[NUM_BUCKETS - 1, h] * LOG2E for h in range(N_HEADS)]

    def scores(j, slot, x):
        h, _ = units[x]
        rows = pl.ds(pl.multiple_of(j * t, t), t)
        s_ref[slot, x] = _dot(k_ref[0, rows, h * V_DIM:(h + 1) * V_DIM], qz_ref[x])

    def accumulate(j, slot, x, kind):
        h, _ = units[x]
        s = s_ref[slot, x]
        const = None
        if kind == 'left':
            const = c_left[h]
        elif kind == 'right':
            const = c_right[h]
        else:
            bias = bias_ref[h, kind]
            s = s + jnp.concatenate([bias, bias], axis=1)
        m_old = m_ref[x]
        m_cur = jnp.max(s, axis=0, keepdims=True)
        if const is not None:
            m_cur = m_cur + const
        m_new = jnp.maximum(m_old, m_cur)
        alpha = jnp.exp2(m_old - m_new)
        shift = m_new if const is None else m_new - const
        p = jnp.exp2(s - shift).astype(jnp.bfloat16)
        vta = jnp.concatenate([vt_ref[0, j, h * V_DIM:(h + 1) * V_DIM, :], ones], axis=0)
        acc_ref[x] = acc_ref[x] * alpha + _dot(vta, p)
        m_ref[x] = m_new

    def pair_kinds(pa):
        def kind(d):
            return d + 1 if abs(d) <= 1 else ('left' if d < 0 else 'right')
        return [[kind(2 * pa + c - u) for u in range(ATTN_SUBTILES)] for c in range(2)]

    def pair_body(a, kinds, last=False):
        for c in range(2):
            j = 2 * a + c
            prefetch = not (last and c == 1)
            for x in range(ATTN_LEAD if prefetch else 0):
                scores(j + 1, 1 - c, x)
            for x in range(n_units):
                if prefetch and x + ATTN_LEAD < n_units:
                    scores(j + 1, 1 - c, x + ATTN_LEAD)
                accumulate(j, c, x, kinds[c][units[x][1]])

    for x in range(n_units):
        scores(0, 0, x)

    def left_loop(a, carry):
        pair_body(a, pair_kinds(-2))
        return carry

    def right_loop(a, carry):
        pair_body(a, pair_kinds(ATTN_SUBTILES))
        return carry

    half = ATTN_SUBTILES // 2
    first = half * ip
    final = n_pairs - 1
    lax.fori_loop(0, jnp.maximum(first - 1, 0), left_loop, 0)
    for pa in range(-1, half + 1):
        a = first + pa
        if (final - pa) % half == 0 and 0 <= (final - pa) // half < n_pairs // half:
            @pl.when(jnp.logical_and(a >= 0, a < final))
            def _():
                pair_body(a, pair_kinds(pa))

            @pl.when(a == final)
            def _():
                pair_body(a, pair_kinds(pa), last=True)
        else:
            @pl.when(jnp.logical_and(a >= 0, a < final))
            def _():
                pair_body(a, pair_kinds(pa))
    lax.fori_loop(jnp.minimum(first + half + 1, final), final, right_loop, 0)

    @pl.when(first + half + 1 <= final)
    def _():
        pair_body(final, pair_kinds(ATTN_SUBTILES), last=True)

    lv = lamv_ref[...]
    lam = (jnp.exp(jnp.sum(lv[0:1] * lv[1:2], axis=-1, keepdims=True))
           - jnp.exp(jnp.sum(lv[2:3] * lv[3:4], axis=-1, keepdims=True)) + LAMBDA_INIT)
    for x, (h, u) in enumerate(units):
        acc = acc_ref[x]
        o1 = acc[:V_DIM, :t] / acc[V_DIM:V_DIM + 1, :t]
        o2 = acc[:V_DIM, t:] / acc[V_DIM:V_DIM + 1, t:]
        o = o1 - lam * o2
        ms = jnp.mean(o * o, axis=0, keepdims=True)
        y = (o * lax.rsqrt(ms + NORM_EPS)).T
        o_ref[0, u * t:(u + 1) * t, h * V_DIM:(h + 1) * V_DIM] = (
            y * g_ref[...] * (1.0 - LAMBDA_INIT)).astype(o_ref.dtype)


def _attention(rel_bias, lamv, subln_g, qt, k, vt, bias):
    b, _, s = qt.shape
    t = ATTN_TILE
    tq = ATTN_SUBTILES * t
    n_chunks = s // t
    n_units = N_HEADS * ATTN_SUBTILES
    smem = pl.BlockSpec(memory_space=pltpu.SMEM)
    return pl.pallas_call(
        functools.partial(_attn_kernel, n_chunks=n_chunks),
        grid=(b, s // tq),
        in_specs=[
            smem,
            pl.BlockSpec((4, HEAD_DIM), lambda bi, qi: (0, 0)),
            pl.BlockSpec((1, V_DIM), lambda bi, qi: (0, 0)),
            pl.BlockSpec((1, D_ATTN, tq), lambda bi, qi: (bi, 0, qi)),
            pl.BlockSpec((1, s, D_ATTN), lambda bi, qi: (bi, 0, 0)),
            pl.BlockSpec((1, n_chunks, D_ATTN, t), lambda bi, qi: (bi, 0, 0, 0)),
            pl.BlockSpec((N_HEADS, 3, t, t), lambda bi, qi: (0, 0, 0, 0)),
        ],
        out_specs=pl.BlockSpec((1, tq, D_ATTN), lambda bi, qi: (bi, qi, 0)),
        out_shape=jax.ShapeDtypeStruct((b, s, D_ATTN), jnp.bfloat16),
        scratch_shapes=[pltpu.VMEM((n_units, V_DIM, 2 * t), jnp.bfloat16),
                        pltpu.VMEM((2, n_units, t, 2 * t), jnp.float32),
                        pltpu.VMEM((n_units, 1, 2 * t), jnp.float32),
                        pltpu.VMEM((n_units, V_DIM + ONES_ROWS, 2 * t), jnp.float32)],
        compiler_params=pltpu.CompilerParams(
            dimension_semantics=("arbitrary", "arbitrary"),
            vmem_limit_bytes=VMEM_LIMIT),
        name="diff_attention",
    )(rel_bias, lamv, subln_g, qt, k, vt, bias)


def _lru_kernel(xr_ref, gr_ref, cw_ref, cb_ref, wg_ref, bg_ref, lam_ref, o_ref,
                xp_ref, af_ref, uf_ref, ab_ref, ub_ref, hfo_ref, hbo_ref, *, seq):
    cb = LRU_CHANNELS
    pad = 8
    xp_ref[0:pad, :] = jnp.zeros((pad, cb), jnp.float32)
    xp_ref[pad + seq:2 * pad + seq, :] = jnp.zeros((pad, cb), jnp.float32)
    xp_ref[pad:pad + seq, :] = xr_ref[0]

    cw = cw_ref[...]
    cbias = cb_ref[...]
    decay = jnp.log1p(jnp.exp(-lam_ref[...])) * (-LRU_C * LOG2E)
    n = LRU_CHUNK
    n_seg = seq // n
    groups = cb // LANES

    def seg_rows(ci):
        return pl.ds(ci, n, stride=n_seg)

    def gate_chunk(ci, carry):
        t0 = pl.multiple_of(ci * n, n)
        blk = xp_ref[pl.ds(t0, n + 2 * pad), :]
        xc = cbias
        for tap in range(CONV_WIDTH):
            lo = pad - CONV_WIDTH // 2 + tap
            xc = xc + blk[lo:lo + n] * cw[tap:tap + 1]
        for g in range(groups):
            cols = slice(g * LANES, (g + 1) * LANES)
            xg = xc[:, cols]
            gates = _dot(xg.astype(jnp.bfloat16), wg_ref[g]) + bg_ref[g]
            for direction, (a_ref, u_ref) in enumerate(((af_ref, uf_ref), (ab_ref, ub_ref))):
                r = jax.nn.sigmoid(gates[:, (2 * direction) * LANES:(2 * direction + 1) * LANES])
                ig = jax.nn.sigmoid(gates[:, (2 * direction + 1) * LANES:(2 * direction + 2) * LANES])
                a = jnp.exp2(r * decay[direction:direction + 1, cols])
                a_ref[g, seg_rows(ci), :] = a
                u_ref[g, seg_rows(ci), :] = jnp.sqrt(1.0 - a * a) * (ig * xg)
        return carry

    lax.fori_loop(0, n_seg, gate_chunk, 0)

    def tiles(k):
        rows_f = pl.ds(pl.multiple_of(k * n_seg, n_seg), n_seg)
        rows_b = pl.ds(pl.multiple_of((n - 1 - k) * n_seg, n_seg), n_seg)
        return rows_f, rows_b

    def ends_body(k, carry):
        rows_f, rows_b = tiles(k)
        out = []
        for g in range(groups):
            hf, pf, hb, pb = carry[g]
            a = af_ref[g, rows_f, :]
            hf = a * hf + uf_ref[g, rows_f, :]
            pf = a * pf
            a = ab_ref[g, rows_b, :]
            hb = a * hb + ub_ref[g, rows_b, :]
            pb = a * pb
            out.append((hf, pf, hb, pb))
        return tuple(out)

    zero = jnp.zeros((n_seg, LANES), jnp.float32)
    one = jnp.ones((n_seg, LANES), jnp.float32)
    ends = lax.fori_loop(0, n, ends_body, tuple((zero, one, zero, one) for _ in range(groups)),
                         unroll=LRU_UNROLL)

    starts = []
    for g in range(groups):
        hf, pf, hb, pb = ends[g]
        state = jnp.zeros((1, LANES), jnp.float32)
        fwd = [state]
        for r in range(1, n_seg):
            state = hf[r - 1:r] + pf[r - 1:r] * state
            fwd.append(state)
        state = jnp.zeros((1, LANES), jnp.float32)
        bwd = [state]
        for r in range(n_seg - 2, -1, -1):
            state = hb[r + 1:r + 2] + pb[r + 1:r + 2] * state
            bwd.insert(0, state)
        starts.append((jnp.concatenate(fwd, axis=0), jnp.concatenate(bwd, axis=0)))

    def scan_body(k, carry):
        rows_f, rows_b = tiles(k)
        out = []
        for g in range(groups):
            hf, hb = carry[g]
            hf = af_ref[g, rows_f, :] * hf + uf_ref[g, rows_f, :]
            hfo_ref[g, rows_f, :] = hf
            hb = ab_ref[g, rows_b, :] * hb + ub_ref[g, rows_b, :]
            hbo_ref[g, rows_b, :] = hb
            out.append((hf, hb))
        return tuple(out)

    lax.fori_loop(0, n, scan_body, tuple(starts), unroll=LRU_UNROLL)

    def out_chunk(ci, carry):
        t0 = pl.multiple_of(ci * n, n)
        for g in range(groups):
            cols = slice(g * LANES, (g + 1) * LANES)
            gate = jax.nn.gelu(gr_ref[0, pl.ds(t0, n), cols])
            o_ref[0, pl.ds(t0, n), cols] = (
                gate * (hfo_ref[g, seg_rows(ci), :] + hbo_ref[g, seg_rows(ci), :])).astype(o_ref.dtype)
        return carry

    lax.fori_loop(0, n_seg, out_chunk, 0)


def _rglru(xr, gr, conv_w, conv_b, wg, bg, lru_lambda):
    b, s, c = xr.shape
    cb = LRU_CHANNELS
    groups = cb // LANES
    seq_spec = pl.BlockSpec((1, s, cb), lambda bi, ci: (bi, 0, ci))
    return pl.pallas_call(
        functools.partial(_lru_kernel, seq=s),
        grid=(b, c // cb),
        in_specs=[
            seq_spec, seq_spec,
            pl.BlockSpec((CONV_WIDTH, cb), lambda bi, ci: (0, ci)),
            pl.BlockSpec((1, cb), lambda bi, ci: (0, ci)),
            pl.BlockSpec((groups, LANES, 4 * LANES), lambda bi, ci: (ci, 0, 0)),
            pl.BlockSpec((groups, 1, 4 * LANES), lambda bi, ci: (ci, 0, 0)),
            pl.BlockSpec((2, cb), lambda bi, ci: (0, ci)),
        ],
        out_specs=seq_spec,
        out_shape=jax.ShapeDtypeStruct((b, s, c), jnp.bfloat16),
        scratch_shapes=[pltpu.VMEM((s + 16, cb), jnp.float32)]
        + [pltpu.VMEM((groups, s, LANES), jnp.float32) for _ in range(6)],
        compiler_params=pltpu.CompilerParams(
            dimension_semantics=("arbitrary", "arbitrary"), vmem_limit_bytes=VMEM_LIMIT),
        name="rglru",
    )(xr, gr, conv_w, conv_b, wg, bg, lru_lambda)


def _lru_gate_weights(w_rg, b_rg, w_ig, b_ig):
    n_groups = D_LRU // LANES
    per = LANES // LRU_BLOCK

    def blockdiag(w):
        w = w.reshape(n_groups, per, LRU_BLOCK, LRU_BLOCK)
        eye = jnp.eye(per, dtype=w.dtype)
        return jnp.einsum('gpde,pq->gpdqe', w, eye).reshape(n_groups, LANES, LANES)

    wg = jnp.concatenate([blockdiag(w_rg[0]), blockdiag(w_ig[0]),
                          blockdiag(w_rg[1]), blockdiag(w_ig[1])], axis=-1)
    bg = jnp.concatenate([b.reshape(n_groups, 1, LANES) for b in (b_rg[0], b_ig[0], b_rg[1], b_ig[1])],
                         axis=-1)
    return wg.astype(jnp.bfloat16), bg


def _ffn_kernel(x_ref, ya_ref, yl_ref, wo_ref, g2_ref, wgate_ref, wup_ref, wdown_ref, gf_ref, o_ref,
                act_ref):
    h1 = x_ref[...] + _dot(ya_ref[...], wo_ref[:D_ATTN, :]) + _dot(yl_ref[...], wo_ref[D_ATTN:, :])
    ms = jnp.mean(h1 * h1, axis=-1, keepdims=True)
    n2 = (h1 * lax.rsqrt(ms + NORM_EPS) * g2_ref[...]).astype(jnp.bfloat16)
    for c in range(D_FF // FFN_CHUNK):
        cols = slice(c * FFN_CHUNK, (c + 1) * FFN_CHUNK)
        gate = _dot(n2, wgate_ref[:, cols])
        up = _dot(n2, wup_ref[:, cols])
        act_ref[:, cols] = (gate * jax.nn.sigmoid(gate) * up).astype(jnp.bfloat16)
    h2 = h1 + _dot(act_ref[...], wdown_ref[...])
    ms = jnp.mean(h2 * h2, axis=-1, keepdims=True)
    o_ref[...] = h2 * lax.rsqrt(ms + NORM_EPS) * gf_ref[...]


def _outproj_ffn(x2, ya, yl, wo, g2, wgate, wup, wdown, gf, tm=512):
    n_tok, d = x2.shape
    once = lambda shape: pl.BlockSpec(shape, lambda ti: (0,) * len(shape),
                                      pipeline_mode=pl.Buffered(1))
    return pl.pallas_call(
        _ffn_kernel,
        grid=(n_tok // tm,),
        in_specs=[
            pl.BlockSpec((tm, d), lambda ti: (ti, 0)),
            pl.BlockSpec((tm, D_ATTN), lambda ti: (ti, 0)),
            pl.BlockSpec((tm, D_LRU), lambda ti: (ti, 0)),
            once((d, d)), once((1, d)),
            once((d, D_FF)), once((d, D_FF)), once((D_FF, d)), once((1, d)),
        ],
        out_specs=pl.BlockSpec((tm, d), lambda ti: (ti, 0)),
        out_shape=jax.ShapeDtypeStruct((n_tok, d), jnp.float32),
        scratch_shapes=[pltpu.VMEM((tm, D_FF), jnp.bfloat16)],
        compiler_params=pltpu.CompilerParams(
            dimension_semantics=("arbitrary",), vmem_limit_bytes=VMEM_LIMIT),
        name="outproj_ffn",
    )(x2, ya, yl, wo, g2, wgate, wup, wdown, gf)


def kernel(x, attn_norm_g, w_in, lambda_q1, lambda_k1, lambda_q2, lambda_k2, subln_g, rel_bias,
           conv_w, conv_b, w_rg, b_rg, w_ig, b_ig, lru_lambda, w_out, ffn_norm_g, w_gate, w_up,
           w_down, final_norm_g):
    b, s, d = x.shape
    bf16 = jnp.bfloat16
    w = w_in[0]
    wqt = w[:, :D_ATTN].T.astype(bf16)
    wk = w[:, D_ATTN:2 * D_ATTN].astype(bf16)
    wvt = w[:, 2 * D_ATTN:3 * D_ATTN].T.astype(bf16)
    wx = w[:, 3 * D_ATTN:3 * D_ATTN + D_LRU].astype(bf16)
    wgr = w[:, 3 * D_ATTN + D_LRU:].astype(bf16)

    qt, k, vt, xr, gr = _inproj(x, attn_norm_g[0][None, :], wqt, wk, wvt, wx, wgr)

    bias = _bias_tiles(rel_bias)
    lamv = jnp.stack([lambda_q1[0], lambda_k1[0], lambda_q2[0], lambda_k2[0]])
    y_attn = _attention(rel_bias, lamv, subln_g[0][None, :], qt, k, vt, bias)

    wg, bg = _lru_gate_weights(w_rg[0], b_rg[0], w_ig[0], b_ig[0])
    y_lru = _rglru(xr, gr, conv_w[0], conv_b[0][None, :], wg, bg, lru_lambda[0])

    out = _outproj_ffn(
        x.reshape(b * s, d), y_attn.reshape(b * s, D_ATTN), y_lru.reshape(b * s, D_LRU),
        w_out[0].astype(bf16), ffn_norm_g[0][None, :], w_gate[0].astype(bf16),
        w_up[0].astype(bf16), w_down[0].astype(bf16), final_norm_g[None, :])
    return out.reshape(b, s, d)
```

```python
import functools
import math

import jax
import jax.numpy as jnp
from jax import lax
from jax.experimental import pallas as pl
from jax.experimental.pallas import tpu as pltpu

D_MODEL = 1024
D_ATTN = 512
D_LRU = 512
HEAD_DIM = 64
N_HEADS = 4
V_DIM = 128
NUM_BUCKETS = 32
LRU_BLOCK = 64
LRU_C = 8.0
CONV_WIDTH = 4
D_FF = 2816
NORM_EPS = 1e-6
LAMBDA_INIT = 0.2
LOG2E = 1.4426950408889634

LANES = 128
ATTN_TILE = 256
ATTN_KEYS = 512
ATTN_SUBTILES = 2
ATTN_LEAD = 3
ONES_ROWS = 16
LRU_CHANNELS = 256
LRU_CHUNK = 512
LRU_UNROLL = 8
FFN_CHUNK = 256
VMEM_LIMIT = 56 * 1024 * 1024

_BUCKET_THRESHOLDS = (12, 16, 23, 32, 46, 64, 91)
assert _BUCKET_THRESHOLDS[-1] <= ATTN_TILE


def _nt_dot(a, b):
    return lax.dot_general(a, b, (((1,), (1,)), ((), ())), preferred_element_type=jnp.float32)


def _dot(a, b):
    return jnp.dot(a, b, preferred_element_type=jnp.float32)


def _bias_kernel(table_ref, out_ref):
    h = pl.program_id(0)
    d = pl.program_id(1)
    t = ATTN_TILE
    kk = lax.broadcasted_iota(jnp.int32, (t, t), 0)
    qq = lax.broadcasted_iota(jnp.int32, (t, t), 1)
    rel = (d - 1) * t + kk - qq
    n = jnp.abs(rel)
    large = jnp.full((t, t), NUM_BUCKETS // 4, jnp.int32)
    for thr in _BUCKET_THRESHOLDS:
        large = large + (n >= thr).astype(jnp.int32)
    bucket = jnp.where(rel > 0, NUM_BUCKETS // 2, 0) + jnp.where(n < NUM_BUCKETS // 4, n, large)
    acc = jnp.zeros((t, t), jnp.float32)
    for b in range(NUM_BUCKETS):
        acc = jnp.where(bucket == b, table_ref[b, h], acc)
    out_ref[0, 0] = acc * LOG2E


def _bias_tiles(rel_bias):
    t = ATTN_TILE
    return pl.pallas_call(
        _bias_kernel,
        grid=(N_HEADS, 3),
        in_specs=[pl.BlockSpec(memory_space=pltpu.SMEM)],
        out_specs=pl.BlockSpec((1, 1, t, t), lambda h, d: (h, d, 0, 0)),
        out_shape=jax.ShapeDtypeStruct((N_HEADS, 3, t, t), jnp.float32),
        name="bias_tiles",
    )(rel_bias)


def _inproj_kernel(x_ref, g_ref, wqt_ref, wk_ref, wvt_ref, wx_ref, wg_ref,
                   qt_ref, k_ref, vt_ref, xr_ref, gr_ref):
    x = x_ref[0]
    ms = jnp.mean(x * x, axis=-1, keepdims=True)
    n = (x * lax.rsqrt(ms + NORM_EPS) * g_ref[...]).astype(jnp.bfloat16)
    qt = _nt_dot(wqt_ref[...], n)
    qt_ref[0] = (qt * (LOG2E / math.sqrt(HEAD_DIM))).astype(jnp.bfloat16)
    k_ref[0] = _dot(n, wk_ref[...]).astype(jnp.bfloat16)
    vt = _nt_dot(wvt_ref[...], n).astype(jnp.bfloat16)
    for c in range(vt_ref.shape[1]):
        vt_ref[0, c] = vt[:, c * ATTN_TILE:(c + 1) * ATTN_TILE]
    xr_ref[0] = _dot(n, wx_ref[...])
    gr_ref[0] = _dot(n, wg_ref[...])


def _inproj(x, g, wqt, wk, wvt, wx, wg, tm=512):
    b, s, d = x.shape
    nc = s // ATTN_TILE
    cpt = tm // ATTN_TILE
    const = lambda shape: pl.BlockSpec(shape, lambda bi, ti: (0,) * len(shape))
    return pl.pallas_call(
        _inproj_kernel,
        grid=(b, s // tm),
        in_specs=[
            pl.BlockSpec((1, tm, d), lambda bi, ti: (bi, ti, 0)),
            const((1, d)),
            const((D_ATTN, d)), const((d, D_ATTN)), const((D_ATTN, d)),
            const((d, D_LRU)), const((d, D_LRU)),
        ],
        out_specs=[
            pl.BlockSpec((1, D_ATTN, tm), lambda bi, ti: (bi, 0, ti)),
            pl.BlockSpec((1, tm, D_ATTN), lambda bi, ti: (bi, ti, 0)),
            pl.BlockSpec((1, cpt, D_ATTN, ATTN_TILE), lambda bi, ti: (bi, ti, 0, 0)),
            pl.BlockSpec((1, tm, D_LRU), lambda bi, ti: (bi, ti, 0)),
            pl.BlockSpec((1, tm, D_LRU), lambda bi, ti: (bi, ti, 0)),
        ],
        out_shape=[
            jax.ShapeDtypeStruct((b, D_ATTN, s), jnp.bfloat16),
            jax.ShapeDtypeStruct((b, s, D_ATTN), jnp.bfloat16),
            jax.ShapeDtypeStruct((b, nc, D_ATTN, ATTN_TILE), jnp.bfloat16),
            jax.ShapeDtypeStruct((b, s, D_LRU), jnp.float32),
            jax.ShapeDtypeStruct((b, s, D_LRU), jnp.float32),
        ],
        compiler_params=pltpu.CompilerParams(
            dimension_semantics=("arbitrary", "arbitrary"), vmem_limit_bytes=VMEM_LIMIT),
        name="inproj",
    )(x, g, wqt, wk, wvt, wx, wg)


def _attn_kernel(table_ref, lamv_ref, g_ref, qt_ref, k_ref, vt_ref, bias_ref, o_ref,
                 qz_ref, s_ref, m_ref, acc_ref, *, n_chunks):
    t = ATTN_TILE
    kc = ATTN_KEYS
    ip = pl.program_id(1)
    n_pairs = n_chunks // 2
    units = [(h, u) for h in range(N_HEADS) for u in range(ATTN_SUBTILES)]
    n_units = len(units)

    z = jnp.zeros((HEAD_DIM, t), jnp.bfloat16)
    for x, (h, u) in enumerate(units):
        qt = qt_ref[0, h * V_DIM:(h + 1) * V_DIM, u * t:(u + 1) * t]
        qz_ref[x] = jnp.concatenate([jnp.concatenate([qt[:HEAD_DIM], z], axis=1),
                                     jnp.concatenate([z, qt[HEAD_DIM:]], axis=1)], axis=0)
    ones = jnp.ones((ONES_ROWS, kc), jnp.bfloat16)
    m_ref[...] = jnp.full(m_ref.shape, -1e30, jnp.float32)
    acc_ref[...] = jnp.zeros(acc_ref.shape, jnp.float32)

    c_left = [table_ref[NUM_BUCKETS // 2 - 1, h] * LOG2E for h in range(N_HEADS)]
    c_right = [table_ref[NUM_BUCKETS - 1, h] * LOG2E for h in range(N_HEADS)]

    def scores(j, slot, x):
        h, _ = units[x]
        rows = pl.ds(pl.multiple_of(j * kc, kc), kc)
        s_ref[slot, x] = _dot(k_ref[0, rows, h * V_DIM:(h + 1) * V_DIM], qz_ref[x])

    def accumulate(j, slot, x, rel, on_left):
        h, u = units[x]
        s = s_ref[slot, x]
        if rel is None:
            const = jnp.where(on_left, c_left[h], c_right[h])
        else:
            halves = [2 * rel - u, 2 * rel - u + 1]
            if halves[1] < -1:
                const = c_left[h]
            elif halves[0] > 1:
                const = c_right[h]
            else:
                const = None
                parts = [bias_ref[h, d + 1] if abs(d) <= 1
                         else jnp.full((t, t), c_left[h] if d < 0 else c_right[h], jnp.float32)
                         for d in halves]
                bias = jnp.concatenate(parts, axis=0)
                s = s + jnp.concatenate([bias, bias], axis=1)
        m_old = m_ref[x]
        m_cur = jnp.max(s, axis=0, keepdims=True)
        if const is not None:
            m_cur = m_cur + const
        m_new = jnp.maximum(m_old, m_cur)
        alpha = jnp.exp2(m_old - m_new)
        shift = m_new if const is None else m_new - const
        p = jnp.exp2(s - shift).astype(jnp.bfloat16)
        hd = slice(h * V_DIM, (h + 1) * V_DIM)
        vta = jnp.concatenate([jnp.concatenate([vt_ref[0, 2 * j, hd, :], vt_ref[0, 2 * j + 1, hd, :]], axis=1),
                               ones], axis=0)
        acc_ref[x] = acc_ref[x] * alpha + _dot(vta, p)
        m_ref[x] = m_new

    def pair_body(a, a_next, rels, on_left=None, last=False):
        for c in range(2):
            j = 2 * a + c
            prefetch = not (last and c == 1)
            j_next = (j + 1 if c == 0 else 2 * a_next) if prefetch else None
            for x in range(ATTN_LEAD if prefetch else 0):
                scores(j_next, 1 - c, x)
            for x in range(n_units):
                if prefetch and x + ATTN_LEAD < n_units:
                    scores(j_next, 1 - c, x + ATTN_LEAD)
                accumulate(j, c, x, None if rels is None else rels[c], on_left)

    odd = lax.rem(ip, 2)
    near_a = (ip - 2 + odd) // 2
    near_b = near_a + 1
    n_left = jnp.maximum(near_a, 0)
    n_right = jnp.maximum(n_pairs - 1 - near_b, 0)
    n_far = n_left + n_right

    def far_pair(i):
        return jnp.where(i < n_left, i, i - n_left + near_b + 1)

    has_a = near_a >= 0
    has_b = near_b < n_pairs
    start = jnp.where(has_a, near_a, near_b)
    for x in range(n_units):
        scores(2 * start, 0, x)

    after_a = jnp.where(has_b, near_b, far_pair(0))
    for parity, rels_a, rels_b in ((0, (-2, -1), (0, 1)), (1, (-1, 0), (1, 2))):
        @pl.when(jnp.logical_and(odd == parity, has_a))
        def _():
            pair_body(near_a, after_a, rels_a)

        @pl.when(jnp.logical_and(odd == parity, has_b))
        def _():
            pair_body(near_b, far_pair(0), rels_b)

    def far_loop(i, carry):
        a = far_pair(i)
        pair_body(a, far_pair(i + 1), None, on_left=a < near_a)
        return carry

    lax.fori_loop(0, n_far - 1, far_loop, 0)
    a_last = far_pair(n_far - 1)
    pair_body(a_last, None, None, on_left=a_last < near_a, last=True)

    lv = lamv_ref[...]
    lam = (jnp.exp(jnp.sum(lv[0:1] * lv[1:2], axis=-1, keepdims=True))
           - jnp.exp(jnp.sum(lv[2:3] * lv[3:4], axis=-1, keepdims=True)) + LAMBDA_INIT)
    for x, (h, u) in enumerate(units):
        acc = acc_ref[x]
        o1 = acc[:V_DIM, :t] / acc[V_DIM:V_DIM + 1, :t]
        o2 = acc[:V_DIM, t:] / acc[V_DIM:V_DIM + 1, t:]
        o = o1 - lam * o2
        ms = jnp.mean(o * o, axis=0, keepdims=True)
        y = (o * lax.rsqrt(ms + NORM_EPS)).T
        o_ref[0, u * t:(u + 1) * t, h * V_DIM:(h + 1) * V_DIM] = (
            y * g_ref[...] * (1.0 - LAMBDA_INIT)).astype(o_ref.dtype)


def _attention(rel_bias, lamv, subln_g, qt, k, vt, bias):
    b, _, s = qt.shape
    t = ATTN_TILE
    tq = ATTN_SUBTILES * t
    n_chunks = s // ATTN_KEYS
    assert n_chunks % 2 == 0 and n_chunks >= 6
    n_units = N_HEADS * ATTN_SUBTILES
    smem = pl.BlockSpec(memory_space=pltpu.SMEM)
    return pl.pallas_call(
        functools.partial(_attn_kernel, n_chunks=n_chunks),
        grid=(b, s // tq),
        in_specs=[
            smem,
            pl.BlockSpec((4, HEAD_DIM), lambda bi, qi: (0, 0)),
            pl.BlockSpec((1, V_DIM), lambda bi, qi: (0, 0)),
            pl.BlockSpec((1, D_ATTN, tq), lambda bi, qi: (bi, 0, qi)),
            pl.BlockSpec((1, s, D_ATTN), lambda bi, qi: (bi, 0, 0)),
            pl.BlockSpec((1, s // t, D_ATTN, t), lambda bi, qi: (bi, 0, 0, 0)),
            pl.BlockSpec((N_HEADS, 3, t, t), lambda bi, qi: (0, 0, 0, 0)),
        ],
        out_specs=pl.BlockSpec((1, tq, D_ATTN), lambda bi, qi: (bi, qi, 0)),
        out_shape=jax.ShapeDtypeStruct((b, s, D_ATTN), jnp.bfloat16),
        scratch_shapes=[pltpu.VMEM((n_units, V_DIM, 2 * t), jnp.bfloat16),
                        pltpu.VMEM((2, n_units, ATTN_KEYS, 2 * t), jnp.float32),
                        pltpu.VMEM((n_units, 1, 2 * t), jnp.float32),
                        pltpu.VMEM((n_units, V_DIM + ONES_ROWS, 2 * t), jnp.float32)],
        compiler_params=pltpu.CompilerParams(
            dimension_semantics=("arbitrary", "arbitrary"),
            vmem_limit_bytes=VMEM_LIMIT),
        name="diff_attention",
    )(rel_bias, lamv, subln_g, qt, k, vt, bias)


def _lru_kernel(xr_ref, gr_ref, cw_ref, cb_ref, wg_ref, bg_ref, lam_ref, o_ref,
                xp_ref, af_ref, uf_ref, ab_ref, ub_ref, hfo_ref, hbo_ref, *, seq):
    cb = LRU_CHANNELS
    pad = 8
    xp_ref[0:pad, :] = jnp.zeros((pad, cb), jnp.float32)
    xp_ref[pad + seq:2 * pad + seq, :] = jnp.zeros((pad, cb), jnp.float32)
    xp_ref[pad:pad + seq, :] = xr_ref[0]

    cw = cw_ref[...]
    cbias = cb_ref[...]
    decay = jnp.log1p(jnp.exp(-lam_ref[...])) * (-LRU_C * LOG2E)
    n = LRU_CHUNK
    n_seg = seq // n
    groups = cb // LANES

    def seg_rows(ci):
        return pl.ds(ci, n, stride=n_seg)

    def gate_chunk(ci, carry):
        t0 = pl.multiple_of(ci * n, n)
        blk = xp_ref[pl.ds(t0, n + 2 * pad), :]
        xc = cbias
        for tap in range(CONV_WIDTH):
            lo = pad - CONV_WIDTH // 2 + tap
            xc = xc + blk[lo:lo + n] * cw[tap:tap + 1]
        for g in range(groups):
            cols = slice(g * LANES, (g + 1) * LANES)
            xg = xc[:, cols]
            gates = _dot(xg.astype(jnp.bfloat16), wg_ref[g]) + bg_ref[g]
            for direction, (a_ref, u_ref) in enumerate(((af_ref, uf_ref), (ab_ref, ub_ref))):
                r = jax.nn.sigmoid(gates[:, (2 * direction) * LANES:(2 * direction + 1) * LANES])
                ig = jax.nn.sigmoid(gates[:, (2 * direction + 1) * LANES:(2 * direction + 2) * LANES])
                a = jnp.exp2(r * decay[direction:direction + 1, cols])
                a_ref[g, seg_rows(ci), :] = a
                u_ref[g, seg_rows(ci), :] = jnp.sqrt(1.0 - a * a) * (ig * xg)
        return carry

    lax.fori_loop(0, n_seg, gate_chunk, 0)

    def tiles(k):
        rows_f = pl.ds(pl.multiple_of(k * n_seg, n_seg), n_seg)
        rows_b = pl.ds(pl.multiple_of((n - 1 - k) * n_seg, n_seg), n_seg)
        return rows_f, rows_b

    def ends_body(k, carry):
        rows_f, rows_b = tiles(k)
        out = []
        for g in range(groups):
            hf, pf, hb, pb = carry[g]
            a = af_ref[g, rows_f, :]
            hf = a * hf + uf_ref[g, rows_f, :]
            pf = a * pf
            a = ab_ref[g, rows_b, :]
            hb = a * hb + ub_ref[g, rows_b, :]
            pb = a * pb
            out.append((hf, pf, hb, pb))
        return tuple(out)

    zero = jnp.zeros((n_seg, LANES), jnp.float32)
    one = jnp.ones((n_seg, LANES), jnp.float32)
    ends = lax.fori_loop(0, n, ends_body, tuple((zero, one, zero, one) for _ in range(groups)),
                         unroll=LRU_UNROLL)

    starts = []
    for g in range(groups):
        hf, pf, hb, pb = ends[g]
        state = jnp.zeros((1, LANES), jnp.float32)
        fwd = [state]
        for r in range(1, n_seg):
            state = hf[r - 1:r] + pf[r - 1:r] * state
            fwd.append(state)
        state = jnp.zeros((1, LANES), jnp.float32)
        bwd = [state]
        for r in range(n_seg - 2, -1, -1):
            state = hb[r + 1:r + 2] + pb[r + 1:r + 2] * state
            bwd.insert(0, state)
        starts.append((jnp.concatenate(fwd, axis=0), jnp.concatenate(bwd, axis=0)))

    def scan_body(k, carry):
        rows_f, rows_b = tiles(k)
        out = []
        for g in range(groups):
            hf, hb = carry[g]
            hf = af_ref[g, rows_f, :] * hf + uf_ref[g, rows_f, :]
            hfo_ref[g, rows_f, :] = hf
            hb = ab_ref[g, rows_b, :] * hb + ub_ref[g, rows_b, :]
            hbo_ref[g, rows_b, :] = hb
            out.append((hf, hb))
        return tuple(out)

    lax.fori_loop(0, n, scan_body, tuple(starts), unroll=LRU_UNROLL)

    def out_chunk(ci, carry):
        t0 = pl.multiple_of(ci * n, n)
        for g in range(groups):
            cols = slice(g * LANES, (g + 1) * LANES)
            gate = jax.nn.gelu(gr_ref[0, pl.ds(t0, n), cols])
            o_ref[0, pl.ds(t0, n), cols] = (
                gate * (hfo_ref[g, seg_rows(ci), :] + hbo_ref[g, seg_rows(ci), :])).astype(o_ref.dtype)
        return carry

    lax.fori_loop(0, n_seg, out_chunk, 0)


def _rglru(xr, gr, conv_w, conv_b, wg, bg, lru_lambda):
    b, s, c = xr.shape
    cb = LRU_CHANNELS
    groups = cb // LANES
    seq_spec = pl.BlockSpec((1, s, cb), lambda bi, ci: (bi, 0, ci))
    return pl.pallas_call(
        functools.partial(_lru_kernel, seq=s),
        grid=(b, c // cb),
        in_specs=[
            seq_spec, seq_spec,
            pl.BlockSpec((CONV_WIDTH, cb), lambda bi, ci: (0, ci)),
            pl.BlockSpec((1, cb), lambda bi, ci: (0, ci)),
            pl.BlockSpec((groups, LANES, 4 * LANES), lambda bi, ci: (ci, 0, 0)),
            pl.BlockSpec((groups, 1, 4 * LANES), lambda bi, ci: (ci, 0, 0)),
            pl.BlockSpec((2, cb), lambda bi, ci: (0, ci)),
        ],
        out_specs=seq_spec,
        out_shape=jax.ShapeDtypeStruct((b, s, c), jnp.bfloat16),
        scratch_shapes=[pltpu.VMEM((s + 16, cb), jnp.float32)]
        + [pltpu.VMEM((groups, s, LANES), jnp.float32) for _ in range(6)],
        compiler_params=pltpu.CompilerParams(
            dimension_semantics=("arbitrary", "arbitrary"), vmem_limit_bytes=VMEM_LIMIT),
        name="rglru",
    )(xr, gr, conv_w, conv_b, wg, bg, lru_lambda)


def _lru_gate_weights(w_rg, b_rg, w_ig, b_ig):
    n_groups = D_LRU // LANES
    per = LANES // LRU_BLOCK

    def blockdiag(w):
        w = w.reshape(n_groups, per, LRU_BLOCK, LRU_BLOCK)
        eye = jnp.eye(per, dtype=w.dtype)
        return jnp.einsum('gpde,pq->gpdqe', w, eye).reshape(n_groups, LANES, LANES)

    wg = jnp.concatenate([blockdiag(w_rg[0]), blockdiag(w_ig[0]),
                          blockdiag(w_rg[1]), blockdiag(w_ig[1])], axis=-1)
    bg = jnp.concatenate([b.reshape(n_groups, 1, LANES) for b in (b_rg[0], b_ig[0], b_rg[1], b_ig[1])],
                         axis=-1)
    return wg.astype(jnp.bfloat16), bg


def _ffn_kernel(x_ref, ya_ref, yl_ref, wo_ref, g2_ref, wgate_ref, wup_ref, wdown_ref, gf_ref, o_ref,
                act_ref):
    h1 = x_ref[...] + _dot(ya_ref[...], wo_ref[:D_ATTN, :]) + _dot(yl_ref[...], wo_ref[D_ATTN:, :])
    ms = jnp.mean(h1 * h1, axis=-1, keepdims=True)
    n2 = (h1 * lax.rsqrt(ms + NORM_EPS) * g2_ref[...]).astype(jnp.bfloat16)
    for c in range(D_FF // FFN_CHUNK):
        cols = slice(c * FFN_CHUNK, (c + 1) * FFN_CHUNK)
        gate = _dot(n2, wgate_ref[:, cols])
        up = _dot(n2, wup_ref[:, cols])
        act_ref[:, cols] = (gate * jax.nn.sigmoid(gate) * up).astype(jnp.bfloat16)
    h2 = h1 + _dot(act_ref[...], wdown_ref[...])
    ms = jnp.mean(h2 * h2, axis=-1, keepdims=True)
    o_ref[...] = h2 * lax.rsqrt(ms + NORM_EPS) * gf_ref[...]


def _outproj_ffn(x2, ya, yl, wo, g2, wgate, wup, wdown, gf, tm=512):
    n_tok, d = x2.shape
    once = lambda shape: pl.BlockSpec(shape, lambda ti: (0,) * len(shape),
                                      pipeline_mode=pl.Buffered(1))
    return pl.pallas_call(
        _ffn_kernel,
        grid=(n_tok // tm,),
        in_specs=[
            pl.BlockSpec((tm, d), lambda ti: (ti, 0)),
            pl.BlockSpec((tm, D_ATTN), lambda ti: (ti, 0)),
            pl.BlockSpec((tm, D_LRU), lambda ti: (ti, 0)),
            once((d, d)), once((1, d)),
            once((d, D_FF)), once((d, D_FF)), once((D_FF, d)), once((1, d)),
        ],
        out_specs=pl.BlockSpec((tm, d), lambda ti: (ti, 0)),
        out_shape=jax.ShapeDtypeStruct((n_tok, d), jnp.float32),
        scratch_shapes=[pltpu.VMEM((tm, D_FF), jnp.bfloat16)],
        compiler_params=pltpu.CompilerParams(
            dimension_semantics=("arbitrary",), vmem_limit_bytes=VMEM_LIMIT),
        name="outproj_ffn",
    )(x2, ya, yl, wo, g2, wgate, wup, wdown, gf)


def kernel(x, attn_norm_g, w_in, lambda_q1, lambda_k1, lambda_q2, lambda_k2, subln_g, rel_bias,
           conv_w, conv_b, w_rg, b_rg, w_ig, b_ig, lru_lambda, w_out, ffn_norm_g, w_gate, w_up,
           w_down, final_norm_g):
    b, s, d = x.shape
    bf16 = jnp.bfloat16
    w = w_in[0]
    wqt = w[:, :D_ATTN].T.astype(bf16)
    wk = w[:, D_ATTN:2 * D_ATTN].astype(bf16)
    wvt = w[:, 2 * D_ATTN:3 * D_ATTN].T.astype(bf16)
    wx = w[:, 3 * D_ATTN:3 * D_ATTN + D_LRU].astype(bf16)
    wgr = w[:, 3 * D_ATTN + D_LRU:].astype(bf16)

    qt, k, vt, xr, gr = _inproj(x, attn_norm_g[0][None, :], wqt, wk, wvt, wx, wgr)

    bias = _bias_tiles(rel_bias)
    lamv = jnp.stack([lambda_q1[0], lambda_k1[0], lambda_q2[0], lambda_k2[0]])
    y_attn = _attention(rel_bias, lamv, subln_g[0][None, :], qt, k, vt, bias)

    wg, bg = _lru_gate_weights(w_rg[0], b_rg[0], w_ig[0], b_ig[0])
    y_lru = _rglru(xr, gr, conv_w[0], conv_b[0][None, :], wg, bg, lru_lambda[0])

    out = _outproj_ffn(
        x.reshape(b * s, d), y_attn.reshape(b * s, D_ATTN), y_lru.reshape(b * s, D_LRU),
        w_out[0].astype(bf16), ffn_norm_g[0][None, :], w_gate[0].astype(bf16),
        w_up[0].astype(bf16), w_down[0].astype(bf16), final_norm_g[None, :])
    return out.reshape(b, s, d)
```

```python
import functools
import math

import jax
import jax.numpy as jnp
from jax import lax
from jax.experimental import pallas as pl
from jax.experimental.pallas import tpu as pltpu

D_MODEL = 1024
D_ATTN = 512
D_LRU = 512
HEAD_DIM = 64
N_HEADS = 4
V_DIM = 128
NUM_BUCKETS = 32
LRU_BLOCK = 64
LRU_C = 8.0
CONV_WIDTH = 4
D_FF = 2816
NORM_EPS = 1e-6
LAMBDA_INIT = 0.2
LOG2E = 1.4426950408889634

LANES = 128
ATTN_TILE = 256
ATTN_KEYS = 512
ATTN_SUBTILES = 4
ATTN_LEAD = 3
ONES_ROWS = 16
LRU_CHANNELS = 256
LRU_CHUNK = 512
LRU_UNROLL = 8
FFN_CHUNK = 256
VMEM_LIMIT = 56 * 1024 * 1024

_BUCKET_THRESHOLDS = (12, 16, 23, 32, 46, 64, 91)
assert _BUCKET_THRESHOLDS[-1] <= ATTN_TILE


def _nt_dot(a, b):
    return lax.dot_general(a, b, (((1,), (1,)), ((), ())), preferred_element_type=jnp.float32)


def _dot(a, b):
    return jnp.dot(a, b, preferred_element_type=jnp.float32)


def _bias_kernel(table_ref, out_ref):
    h = pl.program_id(0)
    d = pl.program_id(1)
    t = ATTN_TILE
    kk = lax.broadcasted_iota(jnp.int32, (t, t), 0)
    qq = lax.broadcasted_iota(jnp.int32, (t, t), 1)
    rel = (d - 1) * t + kk - qq
    n = jnp.abs(rel)
    large = jnp.full((t, t), NUM_BUCKETS // 4, jnp.int32)
    for thr in _BUCKET_THRESHOLDS:
        large = large + (n >= thr).astype(jnp.int32)
    bucket = jnp.where(rel > 0, NUM_BUCKETS // 2, 0) + jnp.where(n < NUM_BUCKETS // 4, n, large)
    acc = jnp.zeros((t, t), jnp.float32)
    for b in range(NUM_BUCKETS):
        acc = jnp.where(bucket == b, table_ref[b, h], acc)
    out_ref[0, 0] = acc * LOG2E


def _bias_tiles(rel_bias):
    t = ATTN_TILE
    return pl.pallas_call(
        _bias_kernel,
        grid=(N_HEADS, 3),
        in_specs=[pl.BlockSpec(memory_space=pltpu.SMEM)],
        out_specs=pl.BlockSpec((1, 1, t, t), lambda h, d: (h, d, 0, 0)),
        out_shape=jax.ShapeDtypeStruct((N_HEADS, 3, t, t), jnp.float32),
        name="bias_tiles",
    )(rel_bias)


def _inproj_kernel(x_ref, g_ref, wqt_ref, wk_ref, wvt_ref, wx_ref, wg_ref,
                   qt_ref, k_ref, vt_ref, xr_ref, gr_ref):
    x = x_ref[0]
    ms = jnp.mean(x * x, axis=-1, keepdims=True)
    n = (x * lax.rsqrt(ms + NORM_EPS) * g_ref[...]).astype(jnp.bfloat16)
    qt = _nt_dot(wqt_ref[...], n)
    qt_ref[0] = (qt * (LOG2E / math.sqrt(HEAD_DIM))).astype(jnp.bfloat16)
    k_ref[0] = _dot(n, wk_ref[...]).astype(jnp.bfloat16)
    vt = _nt_dot(wvt_ref[...], n).astype(jnp.bfloat16)
    for c in range(vt_ref.shape[1]):
        vt_ref[0, c] = vt[:, c * ATTN_TILE:(c + 1) * ATTN_TILE]
    xr_ref[0] = _dot(n, wx_ref[...])
    gr_ref[0] = _dot(n, wg_ref[...])


def _inproj(x, g, wqt, wk, wvt, wx, wg, tm=512):
    b, s, d = x.shape
    nc = s // ATTN_TILE
    cpt = tm // ATTN_TILE
    const = lambda shape: pl.BlockSpec(shape, lambda bi, ti: (0,) * len(shape))
    return pl.pallas_call(
        _inproj_kernel,
        grid=(b, s // tm),
        in_specs=[
            pl.BlockSpec((1, tm, d), lambda bi, ti: (bi, ti, 0)),
            const((1, d)),
            const((D_ATTN, d)), const((d, D_ATTN)), const((D_ATTN, d)),
            const((d, D_LRU)), const((d, D_LRU)),
        ],
        out_specs=[
            pl.BlockSpec((1, D_ATTN, tm), lambda bi, ti: (bi, 0, ti)),
            pl.BlockSpec((1, tm, D_ATTN), lambda bi, ti: (bi, ti, 0)),
            pl.BlockSpec((1, cpt, D_ATTN, ATTN_TILE), lambda bi, ti: (bi, ti, 0, 0)),
            pl.BlockSpec((1, tm, D_LRU), lambda bi, ti: (bi, ti, 0)),
            pl.BlockSpec((1, tm, D_LRU), lambda bi, ti: (bi, ti, 0)),
        ],
        out_shape=[
            jax.ShapeDtypeStruct((b, D_ATTN, s), jnp.bfloat16),
            jax.ShapeDtypeStruct((b, s, D_ATTN), jnp.bfloat16),
            jax.ShapeDtypeStruct((b, nc, D_ATTN, ATTN_TILE), jnp.bfloat16),
            jax.ShapeDtypeStruct((b, s, D_LRU), jnp.float32),
            jax.ShapeDtypeStruct((b, s, D_LRU), jnp.float32),
        ],
        compiler_params=pltpu.CompilerParams(
            dimension_semantics=("arbitrary", "arbitrary"), vmem_limit_bytes=VMEM_LIMIT),
        name="inproj",
    )(x, g, wqt, wk, wvt, wx, wg)


def _attn_kernel(table_ref, lamv_ref, g_ref, qt_ref, k_ref, vt_ref, bias_ref, o_ref,
                 qz_ref, s_ref, m_ref, acc_ref, *, n_chunks):
    t = ATTN_TILE
    kc = ATTN_KEYS
    ip = pl.program_id(1)
    units = [(h, u) for h in range(N_HEADS) for u in range(ATTN_SUBTILES)]
    n_units = len(units)
    ring = ATTN_LEAD + 1
    assert n_units % ring == 0 and ATTN_SUBTILES % 2 == 0

    z = jnp.zeros((HEAD_DIM, t), jnp.bfloat16)
    for x, (h, u) in enumerate(units):
        qt = qt_ref[0, h * V_DIM:(h + 1) * V_DIM, u * t:(u + 1) * t]
        qz_ref[x] = jnp.concatenate([jnp.concatenate([qt[:HEAD_DIM], z], axis=1),
                                     jnp.concatenate([z, qt[HEAD_DIM:]], axis=1)], axis=0)
    ones = jnp.ones((ONES_ROWS, kc), jnp.bfloat16)
    m_ref[...] = jnp.full(m_ref.shape, -1e30, jnp.float32)
    acc_ref[...] = jnp.zeros(acc_ref.shape, jnp.float32)

    c_left = [table_ref[NUM_BUCKETS // 2 - 1, h] * LOG2E for h in range(N_HEADS)]
    c_right = [table_ref[NUM_BUCKETS - 1, h] * LOG2E for h in range(N_HEADS)]

    def scores(j, x):
        h, _ = units[x]
        rows = pl.ds(pl.multiple_of(j * kc, kc), kc)
        s_ref[x % ring] = _dot(k_ref[0, rows, h * V_DIM:(h + 1) * V_DIM], qz_ref[x])

    def accumulate(j, x, rel, on_left):
        h, u = units[x]
        s = s_ref[x % ring]
        if rel is None:
            const = jnp.where(on_left, c_left[h], c_right[h])
        else:
            halves = [2 * rel - u, 2 * rel - u + 1]
            if halves[1] < -1:
                const = c_left[h]
            elif halves[0] > 1:
                const = c_right[h]
            else:
                const = None
                parts = [bias_ref[h, d + 1] if abs(d) <= 1
                         else jnp.full((t, t), c_left[h] if d < 0 else c_right[h], jnp.float32)
                         for d in halves]
                bias = jnp.concatenate(parts, axis=0)
                s = s + jnp.concatenate([bias, bias], axis=1)
        m_old = m_ref[x]
        m_cur = jnp.max(s, axis=0, keepdims=True)
        if const is not None:
            m_cur = m_cur + const
        m_new = jnp.maximum(m_old, m_cur)
        alpha = jnp.exp2(m_old - m_new)
        shift = m_new if const is None else m_new - const
        p = jnp.exp2(s - shift).astype(jnp.bfloat16)
        hd = slice(h * V_DIM, (h + 1) * V_DIM)
        vta = jnp.concatenate([jnp.concatenate([vt_ref[0, 2 * j, hd, :], vt_ref[0, 2 * j + 1, hd, :]], axis=1),
                               ones], axis=0)
        acc_ref[x] = acc_ref[x] * alpha + _dot(vta, p)
        m_ref[x] = m_new

    def chunk_body(j, j_next, rel, on_left=None, last=False):
        for x in range(n_units):
            ahead = x + ATTN_LEAD
            if ahead < n_units:
                scores(j, ahead)
            elif not last:
                scores(j_next, ahead - n_units)
            accumulate(j, x, rel, on_left)

    base = (ATTN_SUBTILES // 2) * ip
    near = (-1, ATTN_SUBTILES // 2)
    lo = jnp.maximum(base + near[0], 0)
    hi = jnp.minimum(base + near[1], n_chunks - 1)
    n_left = lo
    n_far = n_left + (n_chunks - 1 - hi)

    def far_chunk(i):
        return jnp.where(i < n_left, i, i - n_left + hi + 1)

    for x in range(ATTN_LEAD):
        scores(lo, x)

    for rel in range(near[0], near[1] + 1):
        j = base + rel

        @pl.when(jnp.logical_and(j >= lo, j <= hi))
        def _():
            chunk_body(j, jnp.where(j < hi, j + 1, far_chunk(0)), rel)

    def far_loop(i, carry):
        j = far_chunk(i)
        chunk_body(j, far_chunk(i + 1), None, on_left=j < lo)
        return carry

    lax.fori_loop(0, n_far - 1, far_loop, 0)
    j_last = far_chunk(n_far - 1)
    chunk_body(j_last, None, None, on_left=j_last < lo, last=True)

    lv = lamv_ref[...]
    lam = (jnp.exp(jnp.sum(lv[0:1] * lv[1:2], axis=-1, keepdims=True))
           - jnp.exp(jnp.sum(lv[2:3] * lv[3:4], axis=-1, keepdims=True)) + LAMBDA_INIT)
    for x, (h, u) in enumerate(units):
        acc = acc_ref[x]
        o1 = acc[:V_DIM, :t] / acc[V_DIM:V_DIM + 1, :t]
        o2 = acc[:V_DIM, t:] / acc[V_DIM:V_DIM + 1, t:]
        o = o1 - lam * o2
        ms = jnp.mean(o * o, axis=0, keepdims=True)
        y = (o * lax.rsqrt(ms + NORM_EPS)).T
        o_ref[0, u * t:(u + 1) * t, h * V_DIM:(h + 1) * V_DIM] = (
            y * g_ref[...] * (1.0 - LAMBDA_INIT)).astype(o_ref.dtype)


def _attention(rel_bias, lamv, subln_g, qt, k, vt, bias):
    b, _, s = qt.shape
    t = ATTN_TILE
    tq = ATTN_SUBTILES * t
    n_chunks = s // ATTN_KEYS
    assert n_chunks >= ATTN_SUBTILES // 2 + 3
    n_units = N_HEADS * ATTN_SUBTILES
    smem = pl.BlockSpec(memory_space=pltpu.SMEM)
    return pl.pallas_call(
        functools.partial(_attn_kernel, n_chunks=n_chunks),
        grid=(b, s // tq),
        in_specs=[
            smem,
            pl.BlockSpec((4, HEAD_DIM), lambda bi, qi: (0, 0)),
            pl.BlockSpec((1, V_DIM), lambda bi, qi: (0, 0)),
            pl.BlockSpec((1, D_ATTN, tq), lambda bi, qi: (bi, 0, qi)),
            pl.BlockSpec((1, s, D_ATTN), lambda bi, qi: (bi, 0, 0)),
            pl.BlockSpec((1, s // t, D_ATTN, t), lambda bi, qi: (bi, 0, 0, 0)),
            pl.BlockSpec((N_HEADS, 3, t, t), lambda bi, qi: (0, 0, 0, 0)),
        ],
        out_specs=pl.BlockSpec((1, tq, D_ATTN), lambda bi, qi: (bi, qi, 0)),
        out_shape=jax.ShapeDtypeStruct((b, s, D_ATTN), jnp.bfloat16),
        scratch_shapes=[pltpu.VMEM((n_units, V_DIM, 2 * t), jnp.bfloat16),
                        pltpu.VMEM((ATTN_LEAD + 1, ATTN_KEYS, 2 * t), jnp.float32),
                        pltpu.VMEM((n_units, 1, 2 * t), jnp.float32),
                        pltpu.VMEM((n_units, V_DIM + ONES_ROWS, 2 * t), jnp.float32)],
        compiler_params=pltpu.CompilerParams(
            dimension_semantics=("arbitrary", "arbitrary"),
            vmem_limit_bytes=VMEM_LIMIT),
        name="diff_attention",
    )(rel_bias, lamv, subln_g, qt, k, vt, bias)


def _lru_kernel(xr_ref, gr_ref, cw_ref, cb_ref, wg_ref, bg_ref, lam_ref, o_ref,
                xp_ref, af_ref, uf_ref, ab_ref, ub_ref, hfo_ref, hbo_ref, *, seq):
    cb = LRU_CHANNELS
    pad = 8
    xp_ref[0:pad, :] = jnp.zeros((pad, cb), jnp.float32)
    xp_ref[pad + seq:2 * pad + seq, :] = jnp.zeros((pad, cb), jnp.float32)
    xp_ref[pad:pad + seq, :] = xr_ref[0]

    cw = cw_ref[...]
    cbias = cb_ref[...]
    decay = jnp.log1p(jnp.exp(-lam_ref[...])) * (-LRU_C * LOG2E)
    n = LRU_CHUNK
    n_seg = seq // n
    groups = cb // LANES

    def seg_rows(ci):
        return pl.ds(ci, n, stride=n_seg)

    def gate_chunk(ci, carry):
        t0 = pl.multiple_of(ci * n, n)
        blk = xp_ref[pl.ds(t0, n + 2 * pad), :]
        xc = cbias
        for tap in range(CONV_WIDTH):
            lo = pad - CONV_WIDTH // 2 + tap
            xc = xc + blk[lo:lo + n] * cw[tap:tap + 1]
        for g in range(groups):
            cols = slice(g * LANES, (g + 1) * LANES)
            xg = xc[:, cols]
            gates = _dot(xg.astype(jnp.bfloat16), wg_ref[g]) + bg_ref[g]
            for direction, (a_ref, u_ref) in enumerate(((af_ref, uf_ref), (ab_ref, ub_ref))):
                r = jax.nn.sigmoid(gates[:, (2 * direction) * LANES:(2 * direction + 1) * LANES])
                ig = jax.nn.sigmoid(gates[:, (2 * direction + 1) * LANES:(2 * direction + 2) * LANES])
                a = jnp.exp2(r * decay[direction:direction + 1, cols])
                a_ref[g, seg_rows(ci), :] = a
                u_ref[g, seg_rows(ci), :] = jnp.sqrt(1.0 - a * a) * (ig * xg)
        return carry

    lax.fori_loop(0, n_seg, gate_chunk, 0)

    def tiles(k):
        rows_f = pl.ds(pl.multiple_of(k * n_seg, n_seg), n_seg)
        rows_b = pl.ds(pl.multiple_of((n - 1 - k) * n_seg, n_seg), n_seg)
        return rows_f, rows_b

    def ends_body(k, carry):
        rows_f, rows_b = tiles(k)
        out = []
        for g in range(groups):
            hf, pf, hb, pb = carry[g]
            a = af_ref[g, rows_f, :]
            hf = a * hf + uf_ref[g, rows_f, :]
            pf = a * pf
            a = ab_ref[g, rows_b, :]
            hb = a * hb + ub_ref[g, rows_b, :]
            pb = a * pb
            out.append((hf, pf, hb, pb))
        return tuple(out)

    zero = jnp.zeros((n_seg, LANES), jnp.float32)
    one = jnp.ones((n_seg, LANES), jnp.float32)
    ends = lax.fori_loop(0, n, ends_body, tuple((zero, one, zero, one) for _ in range(groups)),
                         unroll=LRU_UNROLL)

    starts = []
    for g in range(groups):
        hf, pf, hb, pb = ends[g]
        state = jnp.zeros((1, LANES), jnp.float32)
        fwd = [state]
        for r in range(1, n_seg):
            state = hf[r - 1:r] + pf[r - 1:r] * state
            fwd.append(state)
        state = jnp.zeros((1, LANES), jnp.float32)
        bwd = [state]
        for r in range(n_seg - 2, -1, -1):
            state = hb[r + 1:r + 2] + pb[r + 1:r + 2] * state
            bwd.insert(0, state)
        starts.append((jnp.concatenate(fwd, axis=0), jnp.concatenate(bwd, axis=0)))

    def scan_body(k, carry):
        rows_f, rows_b = tiles(k)
        out = []
        for g in range(groups):
            hf, hb = carry[g]
            hf = af_ref[g, rows_f, :] * hf + uf_ref[g, rows_f, :]
            hfo_ref[g, rows_f, :] = hf
            hb = ab_ref[g, rows_b, :] * hb + ub_ref[g, rows_b, :]
            hbo_ref[g, rows_b, :] = hb
            out.append((hf, hb))
        return tuple(out)

    lax.fori_loop(0, n, scan_body, tuple(starts), unroll=LRU_UNROLL)

    def out_chunk(ci, carry):
        t0 = pl.multiple_of(ci * n, n)
        for g in range(groups):
            cols = slice(g * LANES, (g + 1) * LANES)
            gate = jax.nn.gelu(gr_ref[0, pl.ds(t0, n), cols])
            o_ref[0, pl.ds(t0, n), cols] = (
                gate * (hfo_ref[g, seg_rows(ci), :] + hbo_ref[g, seg_rows(ci), :])).astype(o_ref.dtype)
        return carry

    lax.fori_loop(0, n_seg, out_chunk, 0)


def _rglru(xr, gr, conv_w, conv_b, wg, bg, lru_lambda):
    b, s, c = xr.shape
    cb = LRU_CHANNELS
    groups = cb // LANES
    seq_spec = pl.BlockSpec((1, s, cb), lambda bi, ci: (bi, 0, ci))
    return pl.pallas_call(
        functools.partial(_lru_kernel, seq=s),
        grid=(b, c // cb),
        in_specs=[
            seq_spec, seq_spec,
            pl.BlockSpec((CONV_WIDTH, cb), lambda bi, ci: (0, ci)),
            pl.BlockSpec((1, cb), lambda bi, ci: (0, ci)),
            pl.BlockSpec((groups, LANES, 4 * LANES), lambda bi, ci: (ci, 0, 0)),
            pl.BlockSpec((groups, 1, 4 * LANES), lambda bi, ci: (ci, 0, 0)),
            pl.BlockSpec((2, cb), lambda bi, ci: (0, ci)),
        ],
        out_specs=seq_spec,
        out_shape=jax.ShapeDtypeStruct((b, s, c), jnp.bfloat16),
        scratch_shapes=[pltpu.VMEM((s + 16, cb), jnp.float32)]
        + [pltpu.VMEM((groups, s, LANES), jnp.float32) for _ in range(6)],
        compiler_params=pltpu.CompilerParams(
            dimension_semantics=("arbitrary", "arbitrary"), vmem_limit_bytes=VMEM_LIMIT),
        name="rglru",
    )(xr, gr, conv_w, conv_b, wg, bg, lru_lambda)


def _lru_gate_weights(w_rg, b_rg, w_ig, b_ig):
    n_groups = D_LRU // LANES
    per = LANES // LRU_BLOCK

    def blockdiag(w):
        w = w.reshape(n_groups, per, LRU_BLOCK, LRU_BLOCK)
        eye = jnp.eye(per, dtype=w.dtype)
        return jnp.einsum('gpde,pq->gpdqe', w, eye).reshape(n_groups, LANES, LANES)

    wg = jnp.concatenate([blockdiag(w_rg[0]), blockdiag(w_ig[0]),
                          blockdiag(w_rg[1]), blockdiag(w_ig[1])], axis=-1)
    bg = jnp.concatenate([b.reshape(n_groups, 1, LANES) for b in (b_rg[0], b_ig[0], b_rg[1], b_ig[1])],
                         axis=-1)
    return wg.astype(jnp.bfloat16), bg


def _ffn_kernel(x_ref, ya_ref, yl_ref, wo_ref, g2_ref, wgate_ref, wup_ref, wdown_ref, gf_ref, o_ref,
                act_ref):
    h1 = x_ref[...] + _dot(ya_ref[...], wo_ref[:D_ATTN, :]) + _dot(yl_ref[...], wo_ref[D_ATTN:, :])
    ms = jnp.mean(h1 * h1, axis=-1, keepdims=True)
    n2 = (h1 * lax.rsqrt(ms + NORM_EPS) * g2_ref[...]).astype(jnp.bfloat16)
    for c in range(D_FF // FFN_CHUNK):
        cols = slice(c * FFN_CHUNK, (c + 1) * FFN_CHUNK)
        gate = _dot(n2, wgate_ref[:, cols])
        up = _dot(n2, wup_ref[:, cols])
        act_ref[:, cols] = (gate * jax.nn.sigmoid(gate) * up).astype(jnp.bfloat16)
    h2 = h1 + _dot(act_ref[...], wdown_ref[...])
    ms = jnp.mean(h2 * h2, axis=-1, keepdims=True)
    o_ref[...] = h2 * lax.rsqrt(ms + NORM_EPS) * gf_ref[...]


def _outproj_ffn(x2, ya, yl, wo, g2, wgate, wup, wdown, gf, tm=512):
    n_tok, d = x2.shape
    once = lambda shape: pl.BlockSpec(shape, lambda ti: (0,) * len(shape),
                                      pipeline_mode=pl.Buffered(1))
    return pl.pallas_call(
        _ffn_kernel,
        grid=(n_tok // tm,),
        in_specs=[
            pl.BlockSpec((tm, d), lambda ti: (ti, 0)),
            pl.BlockSpec((tm, D_ATTN), lambda ti: (ti, 0)),
            pl.BlockSpec((tm, D_LRU), lambda ti: (ti, 0)),
            once((d, d)), once((1, d)),
            once((d, D_FF)), once((d, D_FF)), once((D_FF, d)), once((1, d)),
        ],
        out_specs=pl.BlockSpec((tm, d), lambda ti: (ti, 0)),
        out_shape=jax.ShapeDtypeStruct((n_tok, d), jnp.float32),
        scratch_shapes=[pltpu.VMEM((tm, D_FF), jnp.bfloat16)],
        compiler_params=pltpu.CompilerParams(
            dimension_semantics=("arbitrary",), vmem_limit_bytes=VMEM_LIMIT),
        name="outproj_ffn",
    )(x2, ya, yl, wo, g2, wgate, wup, wdown, gf)


def kernel(x, attn_norm_g, w_in, lambda_q1, lambda_k1, lambda_q2, lambda_k2, subln_g, rel_bias,
           conv_w, conv_b, w_rg, b_rg, w_ig, b_ig, lru_lambda, w_out, ffn_norm_g, w_gate, w_up,
           w_down, final_norm_g):
    b, s, d = x.shape
    bf16 = jnp.bfloat16
    w = w_in[0]
    wqt = w[:, :D_ATTN].T.astype(bf16)
    wk = w[:, D_ATTN:2 * D_ATTN].astype(bf16)
    wvt = w[:, 2 * D_ATTN:3 * D_ATTN].T.astype(bf16)
    wx = w[:, 3 * D_ATTN:3 * D_ATTN + D_LRU].astype(bf16)
    wgr = w[:, 3 * D_ATTN + D_LRU:].astype(bf16)

    qt, k, vt, xr, gr = _inproj(x, attn_norm_g[0][None, :], wqt, wk, wvt, wx, wgr)

    bias = _bias_tiles(rel_bias)
    lamv = jnp.stack([lambda_q1[0], lambda_k1[0], lambda_q2[0], lambda_k2[0]])
    y_attn = _attention(rel_bias, lamv, subln_g[0][None, :], qt, k, vt, bias)

    wg, bg = _lru_gate_weights(w_rg[0], b_rg[0], w_ig[0], b_ig[0])
    y_lru = _rglru(xr, gr, conv_w[0], conv_b[0][None, :], wg, bg, lru_lambda[0])

    out = _outproj_ffn(
        x.reshape(b * s, d), y_attn.reshape(b * s, D_ATTN), y_lru.reshape(b * s, D_LRU),
        w_out[0].astype(bf16), ffn_norm_g[0][None, :], w_gate[0].astype(bf16),
        w_up[0].astype(bf16), w_down[0].astype(bf16), final_norm_g[None, :])
    return out.reshape(b, s, d)
```

```python
import functools
import math

import jax
import jax.numpy as jnp
from jax import lax
from jax.experimental import pallas as pl
from jax.experimental.pallas import tpu as pltpu

D_MODEL = 1024
D_ATTN = 512
D_LRU = 512
HEAD_DIM = 64
N_HEADS = 4
V_DIM = 128
NUM_BUCKETS = 32
LRU_BLOCK = 64
LRU_C = 8.0
CONV_WIDTH = 4
D_FF = 2816
NORM_EPS = 1e-6
LAMBDA_INIT = 0.2
LOG2E = 1.4426950408889634

LANES = 128
ATTN_TILE = 256
ATTN_KEYS = 512
ATTN_SUBTILES = 4
ATTN_LEAD = 2
ATTN_RING = 8
ONES_ROWS = 16
LRU_CHANNELS = 256
LRU_CHUNK = 512
LRU_UNROLL = 8
FFN_CHUNK = 256
VMEM_LIMIT = 56 * 1024 * 1024

_BUCKET_THRESHOLDS = (12, 16, 23, 32, 46, 64, 91)
assert _BUCKET_THRESHOLDS[-1] <= ATTN_TILE


def _nt_dot(a, b):
    return lax.dot_general(a, b, (((1,), (1,)), ((), ())), preferred_element_type=jnp.float32)


def _dot(a, b):
    return jnp.dot(a, b, preferred_element_type=jnp.float32)


def _bias_kernel(table_ref, out_ref):
    h = pl.program_id(0)
    d = pl.program_id(1)
    t = ATTN_TILE
    kk = lax.broadcasted_iota(jnp.int32, (t, t), 0)
    qq = lax.broadcasted_iota(jnp.int32, (t, t), 1)
    rel = (d - 1) * t + kk - qq
    n = jnp.abs(rel)
    large = jnp.full((t, t), NUM_BUCKETS // 4, jnp.int32)
    for thr in _BUCKET_THRESHOLDS:
        large = large + (n >= thr).astype(jnp.int32)
    bucket = jnp.where(rel > 0, NUM_BUCKETS // 2, 0) + jnp.where(n < NUM_BUCKETS // 4, n, large)
    acc = jnp.zeros((t, t), jnp.float32)
    for b in range(NUM_BUCKETS):
        acc = jnp.where(bucket == b, table_ref[b, h], acc)
    out_ref[0, 0] = acc * LOG2E


def _bias_tiles(rel_bias):
    t = ATTN_TILE
    return pl.pallas_call(
        _bias_kernel,
        grid=(N_HEADS, 3),
        in_specs=[pl.BlockSpec(memory_space=pltpu.SMEM)],
        out_specs=pl.BlockSpec((1, 1, t, t), lambda h, d: (h, d, 0, 0)),
        out_shape=jax.ShapeDtypeStruct((N_HEADS, 3, t, t), jnp.float32),
        name="bias_tiles",
    )(rel_bias)


def _inproj_kernel(x_ref, g_ref, wqt_ref, wk_ref, wvt_ref, wx_ref, wg_ref,
                   qt_ref, k_ref, vt_ref, xr_ref, gr_ref):
    x = x_ref[0]
    ms = jnp.mean(x * x, axis=-1, keepdims=True)
    n = (x * lax.rsqrt(ms + NORM_EPS) * g_ref[...]).astype(jnp.bfloat16)
    qt = _nt_dot(wqt_ref[...], n)
    qt_ref[0] = (qt * (LOG2E / math.sqrt(HEAD_DIM))).astype(jnp.bfloat16)
    k_ref[0] = _dot(n, wk_ref[...]).astype(jnp.bfloat16)
    vt = _nt_dot(wvt_ref[...], n).astype(jnp.bfloat16)
    for c in range(vt_ref.shape[1]):
        vt_ref[0, c] = vt[:, c * ATTN_TILE:(c + 1) * ATTN_TILE]
    xr_ref[0] = _dot(n, wx_ref[...])
    gr_ref[0] = _dot(n, wg_ref[...])


def _inproj(x, g, wqt, wk, wvt, wx, wg, tm=512):
    b, s, d = x.shape
    nc = s // ATTN_TILE
    cpt = tm // ATTN_TILE
    const = lambda shape: pl.BlockSpec(shape, lambda bi, ti: (0,) * len(shape))
    return pl.pallas_call(
        _inproj_kernel,
        grid=(b, s // tm),
        in_specs=[
            pl.BlockSpec((1, tm, d), lambda bi, ti: (bi, ti, 0)),
            const((1, d)),
            const((D_ATTN, d)), const((d, D_ATTN)), const((D_ATTN, d)),
            const((d, D_LRU)), const((d, D_LRU)),
        ],
        out_specs=[
            pl.BlockSpec((1, D_ATTN, tm), lambda bi, ti: (bi, 0, ti)),
            pl.BlockSpec((1, tm, D_ATTN), lambda bi, ti: (bi, ti, 0)),
            pl.BlockSpec((1, cpt, D_ATTN, ATTN_TILE), lambda bi, ti: (bi, ti, 0, 0)),
            pl.BlockSpec((1, tm, D_LRU), lambda bi, ti: (bi, ti, 0)),
            pl.BlockSpec((1, tm, D_LRU), lambda bi, ti: (bi, ti, 0)),
        ],
        out_shape=[
            jax.ShapeDtypeStruct((b, D_ATTN, s), jnp.bfloat16),
            jax.ShapeDtypeStruct((b, s, D_ATTN), jnp.bfloat16),
            jax.ShapeDtypeStruct((b, nc, D_ATTN, ATTN_TILE), jnp.bfloat16),
            jax.ShapeDtypeStruct((b, s, D_LRU), jnp.float32),
            jax.ShapeDtypeStruct((b, s, D_LRU), jnp.float32),
        ],
        compiler_params=pltpu.CompilerParams(
            dimension_semantics=("arbitrary", "arbitrary"), vmem_limit_bytes=VMEM_LIMIT),
        name="inproj",
    )(x, g, wqt, wk, wvt, wx, wg)


def _attn_kernel(table_ref, lamv_ref, g_ref, qt_ref, k_ref, vt_ref, bias_ref, o_ref,
                 qz_ref, s_ref, m_ref, acc_ref, *, n_chunks):
    t = ATTN_TILE
    kc = ATTN_KEYS
    ip = pl.program_id(1)
    units = [(h, u) for h in range(N_HEADS) for u in range(ATTN_SUBTILES)]
    n_units = len(units)
    ring = ATTN_RING
    assert n_units % ring == 0 and ring > ATTN_LEAD and ATTN_SUBTILES % 2 == 0

    z = jnp.zeros((HEAD_DIM, t), jnp.bfloat16)
    for x, (h, u) in enumerate(units):
        qt = qt_ref[0, h * V_DIM:(h + 1) * V_DIM, u * t:(u + 1) * t]
        qz_ref[x] = jnp.concatenate([jnp.concatenate([qt[:HEAD_DIM], z], axis=1),
                                     jnp.concatenate([z, qt[HEAD_DIM:]], axis=1)], axis=0)
    ones = jnp.ones((ONES_ROWS, kc), jnp.bfloat16)
    m_ref[...] = jnp.full(m_ref.shape, -1e30, jnp.float32)
    acc_ref[...] = jnp.zeros(acc_ref.shape, jnp.float32)

    c_left = [table_ref[NUM_BUCKETS // 2 - 1, h] * LOG2E for h in range(N_HEADS)]
    c_right = [table_ref[NUM_BUCKETS - 1, h] * LOG2E for h in range(N_HEADS)]

    def scores(j, x):
        h, _ = units[x]
        rows = pl.ds(pl.multiple_of(j * kc, kc), kc)
        s_ref[x % ring] = _dot(k_ref[0, rows, h * V_DIM:(h + 1) * V_DIM], qz_ref[x])

    def accumulate(j, x, rel, on_left):
        h, u = units[x]
        s = s_ref[x % ring]
        if rel is None:
            const = jnp.where(on_left, c_left[h], c_right[h])
        else:
            halves = [2 * rel - u, 2 * rel - u + 1]
            if halves[1] < -1:
                const = c_left[h]
            elif halves[0] > 1:
                const = c_right[h]
            else:
                const = None
                parts = [bias_ref[h, d + 1] if abs(d) <= 1
                         else jnp.full((t, t), c_left[h] if d < 0 else c_right[h], jnp.float32)
                         for d in halves]
                bias = jnp.concatenate(parts, axis=0)
                s = s + jnp.concatenate([bias, bias], axis=1)
        m_old = m_ref[x]
        m_cur = jnp.max(s, axis=0, keepdims=True)
        if const is not None:
            m_cur = m_cur + const
        m_new = jnp.maximum(m_old, m_cur)
        alpha = jnp.exp2(m_old - m_new)
        shift = m_new if const is None else m_new - const
        p = jnp.exp2(s - shift).astype(jnp.bfloat16)
        hd = slice(h * V_DIM, (h + 1) * V_DIM)
        vta = jnp.concatenate([jnp.concatenate([vt_ref[0, 2 * j, hd, :], vt_ref[0, 2 * j + 1, hd, :]], axis=1),
                               ones], axis=0)
        acc_ref[x] = acc_ref[x] * alpha + _dot(vta, p)
        m_ref[x] = m_new

    def chunk_body(j, j_next, rel, on_left=None, last=False):
        for x in range(n_units):
            ahead = x + ATTN_LEAD
            if ahead < n_units:
                scores(j, ahead)
            elif not last:
                scores(j_next, ahead - n_units)
            accumulate(j, x, rel, on_left)

    base = (ATTN_SUBTILES // 2) * ip
    near = (-1, ATTN_SUBTILES // 2)
    lo = jnp.maximum(base + near[0], 0)
    hi = jnp.minimum(base + near[1], n_chunks - 1)
    n_left = lo
    n_far = n_left + (n_chunks - 1 - hi)

    def far_chunk(i):
        return jnp.where(i < n_left, i, i - n_left + hi + 1)

    for x in range(ATTN_LEAD):
        scores(lo, x)

    for rel in range(near[0], near[1] + 1):
        j = base + rel

        @pl.when(jnp.logical_and(j >= lo, j <= hi))
        def _():
            chunk_body(j, jnp.where(j < hi, j + 1, far_chunk(0)), rel)

    def far_loop(i, carry):
        j = far_chunk(i)
        chunk_body(j, far_chunk(i + 1), None, on_left=j < lo)
        return carry

    lax.fori_loop(0, n_far - 1, far_loop, 0)
    j_last = far_chunk(n_far - 1)
    chunk_body(j_last, None, None, on_left=j_last < lo, last=True)

    lv = lamv_ref[...]
    lam = (jnp.exp(jnp.sum(lv[0:1] * lv[1:2], axis=-1, keepdims=True))
           - jnp.exp(jnp.sum(lv[2:3] * lv[3:4], axis=-1, keepdims=True)) + LAMBDA_INIT)
    for x, (h, u) in enumerate(units):
        acc = acc_ref[x]
        o1 = acc[:V_DIM, :t] / acc[V_DIM:V_DIM + 1, :t]
        o2 = acc[:V_DIM, t:] / acc[V_DIM:V_DIM + 1, t:]
        o = o1 - lam * o2
        ms = jnp.mean(o * o, axis=0, keepdims=True)
        y = (o * lax.rsqrt(ms + NORM_EPS)).T
        o_ref[0, u * t:(u + 1) * t, h * V_DIM:(h + 1) * V_DIM] = (
            y * g_ref[...] * (1.0 - LAMBDA_INIT)).astype(o_ref.dtype)


def _attention(rel_bias, lamv, subln_g, qt, k, vt, bias):
    b, _, s = qt.shape
    t = ATTN_TILE
    tq = ATTN_SUBTILES * t
    n_chunks = s // ATTN_KEYS
    assert n_chunks >= ATTN_SUBTILES // 2 + 3
    n_units = N_HEADS * ATTN_SUBTILES
    smem = pl.BlockSpec(memory_space=pltpu.SMEM)
    return pl.pallas_call(
        functools.partial(_attn_kernel, n_chunks=n_chunks),
        grid=(b, s // tq),
        in_specs=[
            smem,
            pl.BlockSpec((4, HEAD_DIM), lambda bi, qi: (0, 0)),
            pl.BlockSpec((1, V_DIM), lambda bi, qi: (0, 0)),
            pl.BlockSpec((1, D_ATTN, tq), lambda bi, qi: (bi, 0, qi)),
            pl.BlockSpec((1, s, D_ATTN), lambda bi, qi: (bi, 0, 0)),
            pl.BlockSpec((1, s // t, D_ATTN, t), lambda bi, qi: (bi, 0, 0, 0)),
            pl.BlockSpec((N_HEADS, 3, t, t), lambda bi, qi: (0, 0, 0, 0)),
        ],
        out_specs=pl.BlockSpec((1, tq, D_ATTN), lambda bi, qi: (bi, qi, 0)),
        out_shape=jax.ShapeDtypeStruct((b, s, D_ATTN), jnp.bfloat16),
        scratch_shapes=[pltpu.VMEM((n_units, V_DIM, 2 * t), jnp.bfloat16),
                        pltpu.VMEM((ATTN_RING, ATTN_KEYS, 2 * t), jnp.float32),
                        pltpu.VMEM((n_units, 1, 2 * t), jnp.float32),
                        pltpu.VMEM((n_units, V_DIM + ONES_ROWS, 2 * t), jnp.float32)],
        compiler_params=pltpu.CompilerParams(
            dimension_semantics=("arbitrary", "arbitrary"),
            vmem_limit_bytes=VMEM_LIMIT),
        name="diff_attention",
    )(rel_bias, lamv, subln_g, qt, k, vt, bias)


def _lru_kernel(xr_ref, gr_ref, cw_ref, cb_ref, wg_ref, bg_ref, lam_ref, o_ref,
                xp_ref, af_ref, uf_ref, ab_ref, ub_ref, hfo_ref, hbo_ref, *, seq):
    cb = LRU_CHANNELS
    pad = 8
    xp_ref[0:pad, :] = jnp.zeros((pad, cb), jnp.float32)
    xp_ref[pad + seq:2 * pad + seq, :] = jnp.zeros((pad, cb), jnp.float32)
    xp_ref[pad:pad + seq, :] = xr_ref[0]

    cw = cw_ref[...]
    cbias = cb_ref[...]
    decay = jnp.log1p(jnp.exp(-lam_ref[...])) * (-LRU_C * LOG2E)
    n = LRU_CHUNK
    n_seg = seq // n
    groups = cb // LANES

    def seg_rows(ci):
        return pl.ds(ci, n, stride=n_seg)

    def gate_chunk(ci, carry):
        t0 = pl.multiple_of(ci * n, n)
        blk = xp_ref[pl.ds(t0, n + 2 * pad), :]
        xc = cbias
        for tap in range(CONV_WIDTH):
            lo = pad - CONV_WIDTH // 2 + tap
            xc = xc + blk[lo:lo + n] * cw[tap:tap + 1]
        for g in range(groups):
            cols = slice(g * LANES, (g + 1) * LANES)
            xg = xc[:, cols]
            gates = _dot(xg.astype(jnp.bfloat16), wg_ref[g]) + bg_ref[g]
            for direction, (a_ref, u_ref) in enumerate(((af_ref, uf_ref), (ab_ref, ub_ref))):
                r = jax.nn.sigmoid(gates[:, (2 * direction) * LANES:(2 * direction + 1) * LANES])
                ig = jax.nn.sigmoid(gates[:, (2 * direction + 1) * LANES:(2 * direction + 2) * LANES])
                a = jnp.exp2(r * decay[direction:direction + 1, cols])
                a_ref[g, seg_rows(ci), :] = a
                u_ref[g, seg_rows(ci), :] = jnp.sqrt(1.0 - a * a) * (ig * xg)
        return carry

    lax.fori_loop(0, n_seg, gate_chunk, 0)

    def tiles(k):
        rows_f = pl.ds(pl.multiple_of(k * n_seg, n_seg), n_seg)
        rows_b = pl.ds(pl.multiple_of((n - 1 - k) * n_seg, n_seg), n_seg)
        return rows_f, rows_b

    def ends_body(k, carry):
        rows_f, rows_b = tiles(k)
        out = []
        for g in range(groups):
            hf, pf, hb, pb = carry[g]
            a = af_ref[g, rows_f, :]
            hf = a * hf + uf_ref[g, rows_f, :]
            pf = a * pf
            a = ab_ref[g, rows_b, :]
            hb = a * hb + ub_ref[g, rows_b, :]
            pb = a * pb
            out.append((hf, pf, hb, pb))
        return tuple(out)

    zero = jnp.zeros((n_seg, LANES), jnp.float32)
    one = jnp.ones((n_seg, LANES), jnp.float32)
    ends = lax.fori_loop(0, n, ends_body, tuple((zero, one, zero, one) for _ in range(groups)),
                         unroll=LRU_UNROLL)

    starts = []
    for g in range(groups):
        hf, pf, hb, pb = ends[g]
        state = jnp.zeros((1, LANES), jnp.float32)
        fwd = [state]
        for r in range(1, n_seg):
            state = hf[r - 1:r] + pf[r - 1:r] * state
            fwd.append(state)
        state = jnp.zeros((1, LANES), jnp.float32)
        bwd = [state]
        for r in range(n_seg - 2, -1, -1):
            state = hb[r + 1:r + 2] + pb[r + 1:r + 2] * state
            bwd.insert(0, state)
        starts.append((jnp.concatenate(fwd, axis=0), jnp.concatenate(bwd, axis=0)))

    def scan_body(k, carry):
        rows_f, rows_b = tiles(k)
        out = []
        for g in range(groups):
            hf, hb = carry[g]
            hf = af_ref[g, rows_f, :] * hf + uf_ref[g, rows_f, :]
            hfo_ref[g, rows_f, :] = hf
            hb = ab_ref[g, rows_b, :] * hb + ub_ref[g, rows_b, :]
            hbo_ref[g, rows_b, :] = hb
            out.append((hf, hb))
        return tuple(out)

    lax.fori_loop(0, n, scan_body, tuple(starts), unroll=LRU_UNROLL)

    def out_chunk(ci, carry):
        t0 = pl.multiple_of(ci * n, n)
        for g in range(groups):
            cols = slice(g * LANES, (g + 1) * LANES)
            gate = jax.nn.gelu(gr_ref[0, pl.ds(t0, n), cols])
            o_ref[0, pl.ds(t0, n), cols] = (
                gate * (hfo_ref[g, seg_rows(ci), :] + hbo_ref[g, seg_rows(ci), :])).astype(o_ref.dtype)
        return carry

    lax.fori_loop(0, n_seg, out_chunk, 0)


def _rglru(xr, gr, conv_w, conv_b, wg, bg, lru_lambda):
    b, s, c = xr.shape
    cb = LRU_CHANNELS
    groups = cb // LANES
    seq_spec = pl.BlockSpec((1, s, cb), lambda bi, ci: (bi, 0, ci))
    return pl.pallas_call(
        functools.partial(_lru_kernel, seq=s),
        grid=(b, c // cb),
        in_specs=[
            seq_spec, seq_spec,
            pl.BlockSpec((CONV_WIDTH, cb), lambda bi, ci: (0, ci)),
            pl.BlockSpec((1, cb), lambda bi, ci: (0, ci)),
            pl.BlockSpec((groups, LANES, 4 * LANES), lambda bi, ci: (ci, 0, 0)),
            pl.BlockSpec((groups, 1, 4 * LANES), lambda bi, ci: (ci, 0, 0)),
            pl.BlockSpec((2, cb), lambda bi, ci: (0, ci)),
        ],
        out_specs=seq_spec,
        out_shape=jax.ShapeDtypeStruct((b, s, c), jnp.bfloat16),
        scratch_shapes=[pltpu.VMEM((s + 16, cb), jnp.float32)]
        + [pltpu.VMEM((groups, s, LANES), jnp.float32) for _ in range(6)],
        compiler_params=pltpu.CompilerParams(
            dimension_semantics=("arbitrary", "arbitrary"), vmem_limit_bytes=VMEM_LIMIT),
        name="rglru",
    )(xr, gr, conv_w, conv_b, wg, bg, lru_lambda)


def _lru_gate_weights(w_rg, b_rg, w_ig, b_ig):
    n_groups = D_LRU // LANES
    per = LANES // LRU_BLOCK

    def blockdiag(w):
        w = w.reshape(n_groups, per, LRU_BLOCK, LRU_BLOCK)
        eye = jnp.eye(per, dtype=w.dtype)
        return jnp.einsum('gpde,pq->gpdqe', w, eye).reshape(n_groups, LANES, LANES)

    wg = jnp.concatenate([blockdiag(w_rg[0]), blockdiag(w_ig[0]),
                          blockdiag(w_rg[1]), blockdiag(w_ig[1])], axis=-1)
    bg = jnp.concatenate([b.reshape(n_groups, 1, LANES) for b in (b_rg[0], b_ig[0], b_rg[1], b_ig[1])],
                         axis=-1)
    return wg.astype(jnp.bfloat16), bg


def _ffn_kernel(x_ref, ya_ref, yl_ref, wo_ref, g2_ref, wgate_ref, wup_ref, wdown_ref, gf_ref, o_ref,
                act_ref):
    h1 = x_ref[...] + _dot(ya_ref[...], wo_ref[:D_ATTN, :]) + _dot(yl_ref[...], wo_ref[D_ATTN:, :])
    ms = jnp.mean(h1 * h1, axis=-1, keepdims=True)
    n2 = (h1 * lax.rsqrt(ms + NORM_EPS) * g2_ref[...]).astype(jnp.bfloat16)
    for c in range(D_FF // FFN_CHUNK):
        cols = slice(c * FFN_CHUNK, (c + 1) * FFN_CHUNK)
        gate = _dot(n2, wgate_ref[:, cols])
        up = _dot(n2, wup_ref[:, cols])
        act_ref[:, cols] = (gate * jax.nn.sigmoid(gate) * up).astype(jnp.bfloat16)
    h2 = h1 + _dot(act_ref[...], wdown_ref[...])
    ms = jnp.mean(h2 * h2, axis=-1, keepdims=True)
    o_ref[...] = h2 * lax.rsqrt(ms + NORM_EPS) * gf_ref[...]


def _outproj_ffn(x2, ya, yl, wo, g2, wgate, wup, wdown, gf, tm=512):
    n_tok, d = x2.shape
    once = lambda shape: pl.BlockSpec(shape, lambda ti: (0,) * len(shape),
                                      pipeline_mode=pl.Buffered(1))
    return pl.pallas_call(
        _ffn_kernel,
        grid=(n_tok // tm,),
        in_specs=[
            pl.BlockSpec((tm, d), lambda ti: (ti, 0)),
            pl.BlockSpec((tm, D_ATTN), lambda ti: (ti, 0)),
            pl.BlockSpec((tm, D_LRU), lambda ti: (ti, 0)),
            once((d, d)), once((1, d)),
            once((d, D_FF)), once((d, D_FF)), once((D_FF, d)), once((1, d)),
        ],
        out_specs=pl.BlockSpec((tm, d), lambda ti: (ti, 0)),
        out_shape=jax.ShapeDtypeStruct((n_tok, d), jnp.float32),
        scratch_shapes=[pltpu.VMEM((tm, D_FF), jnp.bfloat16)],
        compiler_params=pltpu.CompilerParams(
            dimension_semantics=("arbitrary",), vmem_limit_bytes=VMEM_LIMIT),
        name="outproj_ffn",
    )(x2, ya, yl, wo, g2, wgate, wup, wdown, gf)


def kernel(x, attn_norm_g, w_in, lambda_q1, lambda_k1, lambda_q2, lambda_k2, subln_g, rel_bias,
           conv_w, conv_b, w_rg, b_rg, w_ig, b_ig, lru_lambda, w_out, ffn_norm_g, w_gate, w_up,
           w_down, final_norm_g):
    b, s, d = x.shape
    bf16 = jnp.bfloat16
    w = w_in[0]
    wqt = w[:, :D_ATTN].T.astype(bf16)
    wk = w[:, D_ATTN:2 * D_ATTN].astype(bf16)
    wvt = w[:, 2 * D_ATTN:3 * D_ATTN].T.astype(bf16)
    wx = w[:, 3 * D_ATTN:3 * D_ATTN + D_LRU].astype(bf16)
    wgr = w[:, 3 * D_ATTN + D_LRU:].astype(bf16)

    qt, k, vt, xr, gr = _inproj(x, attn_norm_g[0][None, :], wqt, wk, wvt, wx, wgr)

    bias = _bias_tiles(rel_bias)
    lamv = jnp.stack([lambda_q1[0], lambda_k1[0], lambda_q2[0], lambda_k2[0]])
    y_attn = _attention(rel_bias, lamv, subln_g[0][None, :], qt, k, vt, bias)

    wg, bg = _lru_gate_weights(w_rg[0], b_rg[0], w_ig[0], b_ig[0])
    y_lru = _rglru(xr, gr, conv_w[0], conv_b[0][None, :], wg, bg, lru_lambda[0])

    out = _outproj_ffn(
        x.reshape(b * s, d), y_attn.reshape(b * s, D_ATTN), y_lru.reshape(b * s, D_LRU),
        w_out[0].astype(bf16), ffn_norm_g[0][None, :], w_gate[0].astype(bf16),
        w_up[0].astype(bf16), w_down[0].astype(bf16), final_norm_g[None, :])
    return out.reshape(b, s, d)
```

```python
import functools
import math

import jax
import jax.numpy as jnp
from jax import lax
from jax.experimental import pallas as pl
from jax.experimental.pallas import tpu as pltpu

D_MODEL = 1024
D_ATTN = 512
D_LRU = 512
HEAD_DIM = 64
N_HEADS = 4
V_DIM = 128
NUM_BUCKETS = 32
LRU_BLOCK = 64
LRU_C = 8.0
CONV_WIDTH = 4
D_FF = 2816
NORM_EPS = 1e-6
LAMBDA_INIT = 0.2
LOG2E = 1.4426950408889634

LANES = 128
ATTN_TILE = 256
ATTN_KEYS = 512
ATTN_SUBTILES = 4
ATTN_LEAD = 2
ATTN_RING = 8
ONES_ROWS = 16
LRU_CHANNELS = 256
LRU_CHUNK = 512
LRU_UNROLL = 8
FFN_CHUNK = 256
VMEM_LIMIT = 56 * 1024 * 1024

_BUCKET_THRESHOLDS = (12, 16, 23, 32, 46, 64, 91)
assert _BUCKET_THRESHOLDS[-1] <= ATTN_TILE


def _nt_dot(a, b):
    return lax.dot_general(a, b, (((1,), (1,)), ((), ())), preferred_element_type=jnp.float32)


def _dot(a, b):
    return jnp.dot(a, b, preferred_element_type=jnp.float32)


def _bias_kernel(table_ref, out_ref):
    h = pl.program_id(0)
    d = pl.program_id(1)
    t = ATTN_TILE
    kk = lax.broadcasted_iota(jnp.int32, (t, t), 0)
    qq = lax.broadcasted_iota(jnp.int32, (t, t), 1)
    rel = (d - 1) * t + kk - qq
    n = jnp.abs(rel)
    large = jnp.full((t, t), NUM_BUCKETS // 4, jnp.int32)
    for thr in _BUCKET_THRESHOLDS:
        large = large + (n >= thr).astype(jnp.int32)
    bucket = jnp.where(rel > 0, NUM_BUCKETS // 2, 0) + jnp.where(n < NUM_BUCKETS // 4, n, large)
    acc = jnp.zeros((t, t), jnp.float32)
    for b in range(NUM_BUCKETS):
        acc = jnp.where(bucket == b, table_ref[b, h], acc)
    out_ref[0, 0] = acc * LOG2E


def _bias_tiles(rel_bias):
    t = ATTN_TILE
    return pl.pallas_call(
        _bias_kernel,
        grid=(N_HEADS, 3),
        in_specs=[pl.BlockSpec(memory_space=pltpu.SMEM)],
        out_specs=pl.BlockSpec((1, 1, t, t), lambda h, d: (h, d, 0, 0)),
        out_shape=jax.ShapeDtypeStruct((N_HEADS, 3, t, t), jnp.float32),
        name="bias_tiles",
    )(rel_bias)


def _inproj_kernel(x_ref, g_ref, wqt_ref, wk_ref, wvt_ref, wx_ref, wg_ref,
                   qt_ref, k_ref, vt_ref, xr_ref, gr_ref):
    x = x_ref[0]
    ms = jnp.mean(x * x, axis=-1, keepdims=True)
    n = (x * lax.rsqrt(ms + NORM_EPS) * g_ref[...]).astype(jnp.bfloat16)
    qt = _nt_dot(wqt_ref[...], n)
    qt_ref[0] = (qt * (LOG2E / math.sqrt(HEAD_DIM))).astype(jnp.bfloat16)
    k_ref[0] = _dot(n, wk_ref[...]).astype(jnp.bfloat16)
    vt = _nt_dot(wvt_ref[...], n).astype(jnp.bfloat16)
    for c in range(vt_ref.shape[1]):
        vt_ref[0, c] = vt[:, c * ATTN_TILE:(c + 1) * ATTN_TILE]
    xr_ref[0] = _dot(n, wx_ref[...])
    gr_ref[0] = _dot(n, wg_ref[...])


def _inproj(x, g, wqt, wk, wvt, wx, wg, tm=1024):
    b, s, d = x.shape
    nc = s // ATTN_TILE
    cpt = tm // ATTN_TILE
    const = lambda shape: pl.BlockSpec(shape, lambda bi, ti: (0,) * len(shape))
    return pl.pallas_call(
        _inproj_kernel,
        grid=(b, s // tm),
        in_specs=[
            pl.BlockSpec((1, tm, d), lambda bi, ti: (bi, ti, 0)),
            const((1, d)),
            const((D_ATTN, d)), const((d, D_ATTN)), const((D_ATTN, d)),
            const((d, D_LRU)), const((d, D_LRU)),
        ],
        out_specs=[
            pl.BlockSpec((1, D_ATTN, tm), lambda bi, ti: (bi, 0, ti)),
            pl.BlockSpec((1, tm, D_ATTN), lambda bi, ti: (bi, ti, 0)),
            pl.BlockSpec((1, cpt, D_ATTN, ATTN_TILE), lambda bi, ti: (bi, ti, 0, 0)),
            pl.BlockSpec((1, tm, D_LRU), lambda bi, ti: (bi, ti, 0)),
            pl.BlockSpec((1, tm, D_LRU), lambda bi, ti: (bi, ti, 0)),
        ],
        out_shape=[
            jax.ShapeDtypeStruct((b, D_ATTN, s), jnp.bfloat16),
            jax.ShapeDtypeStruct((b, s, D_ATTN), jnp.bfloat16),
            jax.ShapeDtypeStruct((b, nc, D_ATTN, ATTN_TILE), jnp.bfloat16),
            jax.ShapeDtypeStruct((b, s, D_LRU), jnp.float32),
            jax.ShapeDtypeStruct((b, s, D_LRU), jnp.float32),
        ],
        compiler_params=pltpu.CompilerParams(
            dimension_semantics=("arbitrary", "arbitrary"), vmem_limit_bytes=VMEM_LIMIT),
        name="inproj",
    )(x, g, wqt, wk, wvt, wx, wg)


def _attn_kernel(table_ref, lamv_ref, g_ref, qt_ref, k_ref, vt_ref, bias_ref, o_ref,
                 qz_ref, s_ref, m_ref, acc_ref, *, n_chunks):
    t = ATTN_TILE
    kc = ATTN_KEYS
    ip = pl.program_id(1)
    units = [(h, u) for h in range(N_HEADS) for u in range(ATTN_SUBTILES)]
    n_units = len(units)
    ring = ATTN_RING
    assert n_units % ring == 0 and ring > ATTN_LEAD and ATTN_SUBTILES % 2 == 0

    z = jnp.zeros((HEAD_DIM, t), jnp.bfloat16)
    for x, (h, u) in enumerate(units):
        qt = qt_ref[0, h * V_DIM:(h + 1) * V_DIM, u * t:(u + 1) * t]
        qz_ref[x] = jnp.concatenate([jnp.concatenate([qt[:HEAD_DIM], z], axis=1),
                                     jnp.concatenate([z, qt[HEAD_DIM:]], axis=1)], axis=0)
    ones = jnp.ones((ONES_ROWS, kc), jnp.bfloat16)
    m_ref[...] = jnp.full(m_ref.shape, -1e30, jnp.float32)
    acc_ref[...] = jnp.zeros(acc_ref.shape, jnp.float32)

    c_left = [table_ref[NUM_BUCKETS // 2 - 1, h] * LOG2E for h in range(N_HEADS)]
    c_right = [table_ref[NUM_BUCKETS - 1, h] * LOG2E for h in range(N_HEADS)]

    def scores(j, x):
        h, _ = units[x]
        rows = pl.ds(pl.multiple_of(j * kc, kc), kc)
        s_ref[x % ring] = _dot(k_ref[0, rows, h * V_DIM:(h + 1) * V_DIM], qz_ref[x])

    def accumulate(j, x, rel, on_left):
        h, u = units[x]
        s = s_ref[x % ring]
        if rel is None:
            const = jnp.where(on_left, c_left[h], c_right[h])
        else:
            halves = [2 * rel - u, 2 * rel - u + 1]
            if halves[1] < -1:
                const = c_left[h]
            elif halves[0] > 1:
                const = c_right[h]
            else:
                const = None
                parts = [bias_ref[h, d + 1] if abs(d) <= 1
                         else jnp.full((t, t), c_left[h] if d < 0 else c_right[h], jnp.float32)
                         for d in halves]
                bias = jnp.concatenate(parts, axis=0)
                s = s + jnp.concatenate([bias, bias], axis=1)
        m_old = m_ref[x]
        m_cur = jnp.max(s, axis=0, keepdims=True)
        if const is not None:
            m_cur = m_cur + const
        m_new = jnp.maximum(m_old, m_cur)
        alpha = jnp.exp2(m_old - m_new)
        shift = m_new if const is None else m_new - const
        p = jnp.exp2(s - shift).astype(jnp.bfloat16)
        hd = slice(h * V_DIM, (h + 1) * V_DIM)
        vta = jnp.concatenate([jnp.concatenate([vt_ref[0, 2 * j, hd, :], vt_ref[0, 2 * j + 1, hd, :]], axis=1),
                               ones], axis=0)
        acc_ref[x] = acc_ref[x] * alpha + _dot(vta, p)
        m_ref[x] = m_new

    def chunk_body(j, j_next, rel, on_left=None, last=False):
        for x in range(n_units):
            ahead = x + ATTN_LEAD
            if ahead < n_units:
                scores(j, ahead)
            elif not last:
                scores(j_next, ahead - n_units)
            accumulate(j, x, rel, on_left)

    base = (ATTN_SUBTILES // 2) * ip
    near = (-1, ATTN_SUBTILES // 2)
    lo = jnp.maximum(base + near[0], 0)
    hi = jnp.minimum(base + near[1], n_chunks - 1)
    n_left = lo
    n_far = n_left + (n_chunks - 1 - hi)

    def far_chunk(i):
        return jnp.where(i < n_left, i, i - n_left + hi + 1)

    for x in range(ATTN_LEAD):
        scores(lo, x)

    for rel in range(near[0], near[1] + 1):
        j = base + rel

        @pl.when(jnp.logical_and(j >= lo, j <= hi))
        def _():
            chunk_body(j, jnp.where(j < hi, j + 1, far_chunk(0)), rel)

    def far_loop(i, carry):
        j = far_chunk(i)
        chunk_body(j, far_chunk(i + 1), None, on_left=j < lo)
        return carry

    lax.fori_loop(0, n_far - 1, far_loop, 0)
    j_last = far_chunk(n_far - 1)
    chunk_body(j_last, None, None, on_left=j_last < lo, last=True)

    lv = lamv_ref[...]
    lam = (jnp.exp(jnp.sum(lv[0:1] * lv[1:2], axis=-1, keepdims=True))
           - jnp.exp(jnp.sum(lv[2:3] * lv[3:4], axis=-1, keepdims=True)) + LAMBDA_INIT)
    for x, (h, u) in enumerate(units):
        acc = acc_ref[x]
        o1 = acc[:V_DIM, :t] / acc[V_DIM:V_DIM + 1, :t]
        o2 = acc[:V_DIM, t:] / acc[V_DIM:V_DIM + 1, t:]
        o = o1 - lam * o2
        ms = jnp.mean(o * o, axis=0, keepdims=True)
        y = (o * lax.rsqrt(ms + NORM_EPS)).T
        o_ref[0, u * t:(u + 1) * t, h * V_DIM:(h + 1) * V_DIM] = (
            y * g_ref[...] * (1.0 - LAMBDA_INIT)).astype(o_ref.dtype)


def _attention(rel_bias, lamv, subln_g, qt, k, vt, bias):
    b, _, s = qt.shape
    t = ATTN_TILE
    tq = ATTN_SUBTILES * t
    n_chunks = s // ATTN_KEYS
    assert n_chunks >= ATTN_SUBTILES // 2 + 3
    n_units = N_HEADS * ATTN_SUBTILES
    smem = pl.BlockSpec(memory_space=pltpu.SMEM)
    return pl.pallas_call(
        functools.partial(_attn_kernel, n_chunks=n_chunks),
        grid=(b, s // tq),
        in_specs=[
            smem,
            pl.BlockSpec((4, HEAD_DIM), lambda bi, qi: (0, 0)),
            pl.BlockSpec((1, V_DIM), lambda bi, qi: (0, 0)),
            pl.BlockSpec((1, D_ATTN, tq), lambda bi, qi: (bi, 0, qi)),
            pl.BlockSpec((1, s, D_ATTN), lambda bi, qi: (bi, 0, 0)),
            pl.BlockSpec((1, s // t, D_ATTN, t), lambda bi, qi: (bi, 0, 0, 0)),
            pl.BlockSpec((N_HEADS, 3, t, t), lambda bi, qi: (0, 0, 0, 0)),
        ],
        out_specs=pl.BlockSpec((1, tq, D_ATTN), lambda bi, qi: (bi, qi, 0)),
        out_shape=jax.ShapeDtypeStruct((b, s, D_ATTN), jnp.bfloat16),
        scratch_shapes=[pltpu.VMEM((n_units, V_DIM, 2 * t), jnp.bfloat16),
                        pltpu.VMEM((ATTN_RING, ATTN_KEYS, 2 * t), jnp.float32),
                        pltpu.VMEM((n_units, 1, 2 * t), jnp.float32),
                        pltpu.VMEM((n_units, V_DIM + ONES_ROWS, 2 * t), jnp.float32)],
        compiler_params=pltpu.CompilerParams(
            dimension_semantics=("arbitrary", "arbitrary"),
            vmem_limit_bytes=VMEM_LIMIT),
        name="diff_attention",
    )(rel_bias, lamv, subln_g, qt, k, vt, bias)


def _lru_kernel(xr_ref, gr_ref, cw_ref, cb_ref, wg_ref, bg_ref, lam_ref, o_ref,
                xp_ref, af_ref, uf_ref, ab_ref, ub_ref, hfo_ref, hbo_ref, *, seq):
    cb = LRU_CHANNELS
    pad = 8
    xp_ref[0:pad, :] = jnp.zeros((pad, cb), jnp.float32)
    xp_ref[pad + seq:2 * pad + seq, :] = jnp.zeros((pad, cb), jnp.float32)
    xp_ref[pad:pad + seq, :] = xr_ref[0]

    cw = cw_ref[...]
    cbias = cb_ref[...]
    decay = jnp.log1p(jnp.exp(-lam_ref[...])) * (-LRU_C * LOG2E)
    n = LRU_CHUNK
    n_seg = seq // n
    groups = cb // LANES

    def seg_rows(ci):
        return pl.ds(ci, n, stride=n_seg)

    def gate_chunk(ci, carry):
        t0 = pl.multiple_of(ci * n, n)
        blk = xp_ref[pl.ds(t0, n + 2 * pad), :]
        xc = cbias
        for tap in range(CONV_WIDTH):
            lo = pad - CONV_WIDTH // 2 + tap
            xc = xc + blk[lo:lo + n] * cw[tap:tap + 1]
        for g in range(groups):
            cols = slice(g * LANES, (g + 1) * LANES)
            xg = xc[:, cols]
            gates = _dot(xg.astype(jnp.bfloat16), wg_ref[g]) + bg_ref[g]
            for direction, (a_ref, u_ref) in enumerate(((af_ref, uf_ref), (ab_ref, ub_ref))):
                r = jax.nn.sigmoid(gates[:, (2 * direction) * LANES:(2 * direction + 1) * LANES])
                ig = jax.nn.sigmoid(gates[:, (2 * direction + 1) * LANES:(2 * direction + 2) * LANES])
                a = jnp.exp2(r * decay[direction:direction + 1, cols])
                a_ref[g, seg_rows(ci), :] = a
                u_ref[g, seg_rows(ci), :] = jnp.sqrt(1.0 - a * a) * (ig * xg)
        return carry

    lax.fori_loop(0, n_seg, gate_chunk, 0)

    def tiles(k):
        rows_f = pl.ds(pl.multiple_of(k * n_seg, n_seg), n_seg)
        rows_b = pl.ds(pl.multiple_of((n - 1 - k) * n_seg, n_seg), n_seg)
        return rows_f, rows_b

    def ends_body(k, carry):
        rows_f, rows_b = tiles(k)
        out = []
        for g in range(groups):
            hf, pf, hb, pb = carry[g]
            a = af_ref[g, rows_f, :]
            hf = a * hf + uf_ref[g, rows_f, :]
            pf = a * pf
            a = ab_ref[g, rows_b, :]
            hb = a * hb + ub_ref[g, rows_b, :]
            pb = a * pb
            out.append((hf, pf, hb, pb))
        return tuple(out)

    zero = jnp.zeros((n_seg, LANES), jnp.float32)
    one = jnp.ones((n_seg, LANES), jnp.float32)
    ends = lax.fori_loop(0, n, ends_body, tuple((zero, one, zero, one) for _ in range(groups)),
                         unroll=LRU_UNROLL)

    starts = []
    for g in range(groups):
        hf, pf, hb, pb = ends[g]
        state = jnp.zeros((1, LANES), jnp.float32)
        fwd = [state]
        for r in range(1, n_seg):
            state = hf[r - 1:r] + pf[r - 1:r] * state
            fwd.append(state)
        state = jnp.zeros((1, LANES), jnp.float32)
        bwd = [state]
        for r in range(n_seg - 2, -1, -1):
            state = hb[r + 1:r + 2] + pb[r + 1:r + 2] * state
            bwd.insert(0, state)
        starts.append((jnp.concatenate(fwd, axis=0), jnp.concatenate(bwd, axis=0)))

    def scan_body(k, carry):
        rows_f, rows_b = tiles(k)
        out = []
        for g in range(groups):
            hf, hb = carry[g]
            hf = af_ref[g, rows_f, :] * hf + uf_ref[g, rows_f, :]
            hfo_ref[g, rows_f, :] = hf
            hb = ab_ref[g, rows_b, :] * hb + ub_ref[g, rows_b, :]
            hbo_ref[g, rows_b, :] = hb
            out.append((hf, hb))
        return tuple(out)

    lax.fori_loop(0, n, scan_body, tuple(starts), unroll=LRU_UNROLL)

    def out_chunk(ci, carry):
        t0 = pl.multiple_of(ci * n, n)
        for g in range(groups):
            cols = slice(g * LANES, (g + 1) * LANES)
            gate = jax.nn.gelu(gr_ref[0, pl.ds(t0, n), cols])
            o_ref[0, pl.ds(t0, n), cols] = (
                gate * (hfo_ref[g, seg_rows(ci), :] + hbo_ref[g, seg_rows(ci), :])).astype(o_ref.dtype)
        return carry

    lax.fori_loop(0, n_seg, out_chunk, 0)


def _rglru(xr, gr, conv_w, conv_b, wg, bg, lru_lambda):
    b, s, c = xr.shape
    cb = LRU_CHANNELS
    groups = cb // LANES
    seq_spec = pl.BlockSpec((1, s, cb), lambda bi, ci: (bi, 0, ci))
    return pl.pallas_call(
        functools.partial(_lru_kernel, seq=s),
        grid=(b, c // cb),
        in_specs=[
            seq_spec, seq_spec,
            pl.BlockSpec((CONV_WIDTH, cb), lambda bi, ci: (0, ci)),
            pl.BlockSpec((1, cb), lambda bi, ci: (0, ci)),
            pl.BlockSpec((groups, LANES, 4 * LANES), lambda bi, ci: (ci, 0, 0)),
            pl.BlockSpec((groups, 1, 4 * LANES), lambda bi, ci: (ci, 0, 0)),
            pl.BlockSpec((2, cb), lambda bi, ci: (0, ci)),
        ],
        out_specs=seq_spec,
        out_shape=jax.ShapeDtypeStruct((b, s, c), jnp.bfloat16),
        scratch_shapes=[pltpu.VMEM((s + 16, cb), jnp.float32)]
        + [pltpu.VMEM((groups, s, LANES), jnp.float32) for _ in range(6)],
        compiler_params=pltpu.CompilerParams(
            dimension_semantics=("arbitrary", "arbitrary"), vmem_limit_bytes=VMEM_LIMIT),
        name="rglru",
    )(xr, gr, conv_w, conv_b, wg, bg, lru_lambda)


def _lru_gate_weights(w_rg, b_rg, w_ig, b_ig):
    n_groups = D_LRU // LANES
    per = LANES // LRU_BLOCK

    def blockdiag(w):
        w = w.reshape(n_groups, per, LRU_BLOCK, LRU_BLOCK)
        eye = jnp.eye(per, dtype=w.dtype)
        return jnp.einsum('gpde,pq->gpdqe', w, eye).reshape(n_groups, LANES, LANES)

    wg = jnp.concatenate([blockdiag(w_rg[0]), blockdiag(w_ig[0]),
                          blockdiag(w_rg[1]), blockdiag(w_ig[1])], axis=-1)
    bg = jnp.concatenate([b.reshape(n_groups, 1, LANES) for b in (b_rg[0], b_ig[0], b_rg[1], b_ig[1])],
                         axis=-1)
    return wg.astype(jnp.bfloat16), bg


def _ffn_kernel(x_ref, ya_ref, yl_ref, wo_ref, g2_ref, wgate_ref, wup_ref, wdown_ref, gf_ref, o_ref,
                act_ref):
    half = x_ref.shape[0] // 2
    rows = [slice(0, half), slice(half, 2 * half)]

    def out_proj(r):
        return (x_ref[r, :] + _dot(ya_ref[r, :], wo_ref[:D_ATTN, :])
                + _dot(yl_ref[r, :], wo_ref[D_ATTN:, :]))

    def normed(h, g_ref):
        ms = jnp.mean(h * h, axis=-1, keepdims=True)
        return h * lax.rsqrt(ms + NORM_EPS) * g_ref[...]

    def swiglu(r, n2):
        for c in range(D_FF // FFN_CHUNK):
            cols = slice(c * FFN_CHUNK, (c + 1) * FFN_CHUNK)
            gate = _dot(n2, wgate_ref[:, cols])
            up = _dot(n2, wup_ref[:, cols])
            act_ref[r, cols] = (gate * jax.nn.sigmoid(gate) * up).astype(jnp.bfloat16)

    h1 = [out_proj(r) for r in rows]
    swiglu(rows[0], normed(h1[0], g2_ref).astype(jnp.bfloat16))
    n2_b = normed(h1[1], g2_ref).astype(jnp.bfloat16)
    h2_a = h1[0] + _dot(act_ref[rows[0], :], wdown_ref[...])
    swiglu(rows[1], n2_b)
    o_ref[rows[0], :] = normed(h2_a, gf_ref)
    h2_b = h1[1] + _dot(act_ref[rows[1], :], wdown_ref[...])
    o_ref[rows[1], :] = normed(h2_b, gf_ref)


def _outproj_ffn(x2, ya, yl, wo, g2, wgate, wup, wdown, gf, tm=512):
    n_tok, d = x2.shape
    once = lambda shape: pl.BlockSpec(shape, lambda ti: (0,) * len(shape),
                                      pipeline_mode=pl.Buffered(1))
    return pl.pallas_call(
        _ffn_kernel,
        grid=(n_tok // tm,),
        in_specs=[
            pl.BlockSpec((tm, d), lambda ti: (ti, 0)),
            pl.BlockSpec((tm, D_ATTN), lambda ti: (ti, 0)),
            pl.BlockSpec((tm, D_LRU), lambda ti: (ti, 0)),
            once((d, d)), once((1, d)),
            once((d, D_FF)), once((d, D_FF)), once((D_FF, d)), once((1, d)),
        ],
        out_specs=pl.BlockSpec((tm, d), lambda ti: (ti, 0)),
        out_shape=jax.ShapeDtypeStruct((n_tok, d), jnp.float32),
        scratch_shapes=[pltpu.VMEM((tm, D_FF), jnp.bfloat16)],
        compiler_params=pltpu.CompilerParams(
            dimension_semantics=("arbitrary",), vmem_limit_bytes=VMEM_LIMIT),
        name="outproj_ffn",
    )(x2, ya, yl, wo, g2, wgate, wup, wdown, gf)


def kernel(x, attn_norm_g, w_in, lambda_q1, lambda_k1, lambda_q2, lambda_k2, subln_g, rel_bias,
           conv_w, conv_b, w_rg, b_rg, w_ig, b_ig, lru_lambda, w_out, ffn_norm_g, w_gate, w_up,
           w_down, final_norm_g):
    b, s, d = x.shape
    bf16 = jnp.bfloat16
    w = w_in[0]
    wqt = w[:, :D_ATTN].T.astype(bf16)
    wk = w[:, D_ATTN:2 * D_ATTN].astype(bf16)
    wvt = w[:, 2 * D_ATTN:3 * D_ATTN].T.astype(bf16)
    wx = w[:, 3 * D_ATTN:3 * D_ATTN + D_LRU].astype(bf16)
    wgr = w[:, 3 * D_ATTN + D_LRU:].astype(bf16)

    qt, k, vt, xr, gr = _inproj(x, attn_norm_g[0][None, :], wqt, wk, wvt, wx, wgr)

    bias = _bias_tiles(rel_bias)
    lamv = jnp.stack([lambda_q1[0], lambda_k1[0], lambda_q2[0], lambda_k2[0]])
    y_attn = _attention(rel_bias, lamv, subln_g[0][None, :], qt, k, vt, bias)

    wg, bg = _lru_gate_weights(w_rg[0], b_rg[0], w_ig[0], b_ig[0])
    y_lru = _rglru(xr, gr, conv_w[0], conv_b[0][None, :], wg, bg, lru_lambda[0])

    out = _outproj_ffn(
        x.reshape(b * s, d), y_attn.reshape(b * s, D_ATTN), y_lru.reshape(b * s, D_LRU),
        w_out[0].astype(bf16), ffn_norm_g[0][None, :], w_gate[0].astype(bf16),
        w_up[0].astype(bf16), w_down[0].astype(bf16), final_norm_g[None, :])
    return out.reshape(b, s, d)
```

```python
import functools
import math

import jax
import jax.numpy as jnp
from jax import lax
from jax.experimental import pallas as pl
from jax.experimental.pallas import tpu as pltpu

D_MODEL = 1024
D_ATTN = 512
D_LRU = 512
HEAD_DIM = 64
N_HEADS = 4
V_DIM = 128
NUM_BUCKETS = 32
LRU_BLOCK = 64
LRU_C = 8.0
CONV_WIDTH = 4
D_FF = 2816
NORM_EPS = 1e-6
LAMBDA_INIT = 0.2
LOG2E = 1.4426950408889634

LANES = 128
ATTN_TILE = 256
ATTN_KEYS = 512
ATTN_SUBTILES = 4
ATTN_LEAD = 2
ATTN_RING = 8
ONES_ROWS = 16
LRU_CHANNELS = 256
LRU_CHUNK = 512
LRU_UNROLL = 8
FFN_CHUNK = 256
FFN_PART_ROWS = 256
VMEM_LIMIT = 56 * 1024 * 1024

_BUCKET_THRESHOLDS = (12, 16, 23, 32, 46, 64, 91)
assert _BUCKET_THRESHOLDS[-1] <= ATTN_TILE


def _nt_dot(a, b):
    return lax.dot_general(a, b, (((1,), (1,)), ((), ())), preferred_element_type=jnp.float32)


def _dot(a, b):
    return jnp.dot(a, b, preferred_element_type=jnp.float32)


def _bias_kernel(table_ref, out_ref):
    h = pl.program_id(0)
    d = pl.program_id(1)
    t = ATTN_TILE
    kk = lax.broadcasted_iota(jnp.int32, (t, t), 0)
    qq = lax.broadcasted_iota(jnp.int32, (t, t), 1)
    rel = (d - 1) * t + kk - qq
    n = jnp.abs(rel)
    large = jnp.full((t, t), NUM_BUCKETS // 4, jnp.int32)
    for thr in _BUCKET_THRESHOLDS:
        large = large + (n >= thr).astype(jnp.int32)
    bucket = jnp.where(rel > 0, NUM_BUCKETS // 2, 0) + jnp.where(n < NUM_BUCKETS // 4, n, large)
    acc = jnp.zeros((t, t), jnp.float32)
    for b in range(NUM_BUCKETS):
        acc = jnp.where(bucket == b, table_ref[b, h], acc)
    out_ref[0, 0] = acc * LOG2E


def _bias_tiles(rel_bias):
    t = ATTN_TILE
    return pl.pallas_call(
        _bias_kernel,
        grid=(N_HEADS, 3),
        in_specs=[pl.BlockSpec(memory_space=pltpu.SMEM)],
        out_specs=pl.BlockSpec((1, 1, t, t), lambda h, d: (h, d, 0, 0)),
        out_shape=jax.ShapeDtypeStruct((N_HEADS, 3, t, t), jnp.float32),
        name="bias_tiles",
    )(rel_bias)


def _inproj_kernel(x_ref, g_ref, wqt_ref, wk_ref, wvt_ref, wx_ref, wg_ref,
                   qt_ref, k_ref, vt_ref, xr_ref, gr_ref):
    x = x_ref[0]
    ms = jnp.mean(x * x, axis=-1, keepdims=True)
    n = (x * lax.rsqrt(ms + NORM_EPS) * g_ref[...]).astype(jnp.bfloat16)
    qt = _nt_dot(wqt_ref[...], n)
    qt_ref[0] = (qt * (LOG2E / math.sqrt(HEAD_DIM))).astype(jnp.bfloat16)
    k_ref[0] = _dot(n, wk_ref[...]).astype(jnp.bfloat16)
    vt = _nt_dot(wvt_ref[...], n).astype(jnp.bfloat16)
    for c in range(vt_ref.shape[1]):
        vt_ref[0, c] = vt[:, c * ATTN_TILE:(c + 1) * ATTN_TILE]
    xr_ref[0] = _dot(n, wx_ref[...])
    gr_ref[0] = _dot(n, wg_ref[...])


def _inproj(x, g, wqt, wk, wvt, wx, wg, tm=1024):
    b, s, d = x.shape
    nc = s // ATTN_TILE
    cpt = tm // ATTN_TILE
    const = lambda shape: pl.BlockSpec(shape, lambda bi, ti: (0,) * len(shape))
    return pl.pallas_call(
        _inproj_kernel,
        grid=(b, s // tm),
        in_specs=[
            pl.BlockSpec((1, tm, d), lambda bi, ti: (bi, ti, 0)),
            const((1, d)),
            const((D_ATTN, d)), const((d, D_ATTN)), const((D_ATTN, d)),
            const((d, D_LRU)), const((d, D_LRU)),
        ],
        out_specs=[
            pl.BlockSpec((1, D_ATTN, tm), lambda bi, ti: (bi, 0, ti)),
            pl.BlockSpec((1, tm, D_ATTN), lambda bi, ti: (bi, ti, 0)),
            pl.BlockSpec((1, cpt, D_ATTN, ATTN_TILE), lambda bi, ti: (bi, ti, 0, 0)),
            pl.BlockSpec((1, tm, D_LRU), lambda bi, ti: (bi, ti, 0)),
            pl.BlockSpec((1, tm, D_LRU), lambda bi, ti: (bi, ti, 0)),
        ],
        out_shape=[
            jax.ShapeDtypeStruct((b, D_ATTN, s), jnp.bfloat16),
            jax.ShapeDtypeStruct((b, s, D_ATTN), jnp.bfloat16),
            jax.ShapeDtypeStruct((b, nc, D_ATTN, ATTN_TILE), jnp.bfloat16),
            jax.ShapeDtypeStruct((b, s, D_LRU), jnp.float32),
            jax.ShapeDtypeStruct((b, s, D_LRU), jnp.float32),
        ],
        compiler_params=pltpu.CompilerParams(
            dimension_semantics=("arbitrary", "arbitrary"), vmem_limit_bytes=VMEM_LIMIT),
        name="inproj",
    )(x, g, wqt, wk, wvt, wx, wg)


def _attn_kernel(table_ref, lamv_ref, g_ref, qt_ref, k_ref, vt_ref, bias_ref, o_ref,
                 qz_ref, s_ref, m_ref, acc_ref, *, n_chunks):
    t = ATTN_TILE
    kc = ATTN_KEYS
    ip = pl.program_id(1)
    units = [(h, u) for h in range(N_HEADS) for u in range(ATTN_SUBTILES)]
    n_units = len(units)
    ring = ATTN_RING
    assert n_units % ring == 0 and ring > ATTN_LEAD and ATTN_SUBTILES % 2 == 0

    z = jnp.zeros((HEAD_DIM, t), jnp.bfloat16)
    for x, (h, u) in enumerate(units):
        qt = qt_ref[0, h * V_DIM:(h + 1) * V_DIM, u * t:(u + 1) * t]
        qz_ref[x] = jnp.concatenate([jnp.concatenate([qt[:HEAD_DIM], z], axis=1),
                                     jnp.concatenate([z, qt[HEAD_DIM:]], axis=1)], axis=0)
    ones = jnp.ones((ONES_ROWS, kc), jnp.bfloat16)
    m_ref[...] = jnp.full(m_ref.shape, -1e30, jnp.float32)
    acc_ref[...] = jnp.zeros(acc_ref.shape, jnp.float32)

    c_left = [table_ref[NUM_BUCKETS // 2 - 1, h] * LOG2E for h in range(N_HEADS)]
    c_right = [table_ref[NUM_BUCKETS - 1, h] * LOG2E for h in range(N_HEADS)]

    def scores(j, x):
        h, _ = units[x]
        rows = pl.ds(pl.multiple_of(j * kc, kc), kc)
        s_ref[x % ring] = _dot(k_ref[0, rows, h * V_DIM:(h + 1) * V_DIM], qz_ref[x])

    def accumulate(j, x, rel, on_left):
        h, u = units[x]
        s = s_ref[x % ring]
        if rel is None:
            const = jnp.where(on_left, c_left[h], c_right[h])
        else:
            halves = [2 * rel - u, 2 * rel - u + 1]
            if halves[1] < -1:
                const = c_left[h]
            elif halves[0] > 1:
                const = c_right[h]
            else:
                const = None
                parts = [bias_ref[h, d + 1] if abs(d) <= 1
                         else jnp.full((t, t), c_left[h] if d < 0 else c_right[h], jnp.float32)
                         for d in halves]
                bias = jnp.concatenate(parts, axis=0)
                s = s + jnp.concatenate([bias, bias], axis=1)
        m_old = m_ref[x]
        m_cur = jnp.max(s, axis=0, keepdims=True)
        if const is not None:
            m_cur = m_cur + const
        m_new = jnp.maximum(m_old, m_cur)
        alpha = jnp.exp2(m_old - m_new)
        shift = m_new if const is None else m_new - const
        p = jnp.exp2(s - shift).astype(jnp.bfloat16)
        hd = slice(h * V_DIM, (h + 1) * V_DIM)
        vta = jnp.concatenate([jnp.concatenate([vt_ref[0, 2 * j, hd, :], vt_ref[0, 2 * j + 1, hd, :]], axis=1),
                               ones], axis=0)
        acc_ref[x] = acc_ref[x] * alpha + _dot(vta, p)
        m_ref[x] = m_new

    def chunk_body(j, j_next, rel, on_left=None, last=False):
        for x in range(n_units):
            ahead = x + ATTN_LEAD
            if ahead < n_units:
                scores(j, ahead)
            elif not last:
                scores(j_next, ahead - n_units)
            accumulate(j, x, rel, on_left)

    base = (ATTN_SUBTILES // 2) * ip
    near = (-1, ATTN_SUBTILES // 2)
    lo = jnp.maximum(base + near[0], 0)
    hi = jnp.minimum(base + near[1], n_chunks - 1)
    n_left = lo
    n_far = n_left + (n_chunks - 1 - hi)

    def far_chunk(i):
        return jnp.where(i < n_left, i, i - n_left + hi + 1)

    for x in range(ATTN_LEAD):
        scores(lo, x)

    for rel in range(near[0], near[1] + 1):
        j = base + rel

        @pl.when(jnp.logical_and(j >= lo, j <= hi))
        def _():
            chunk_body(j, jnp.where(j < hi, j + 1, far_chunk(0)), rel)

    def far_loop(i, carry):
        j = far_chunk(i)
        chunk_body(j, far_chunk(i + 1), None, on_left=j < lo)
        return carry

    lax.fori_loop(0, n_far - 1, far_loop, 0)
    j_last = far_chunk(n_far - 1)
    chunk_body(j_last, None, None, on_left=j_last < lo, last=True)

    lv = lamv_ref[...]
    lam = (jnp.exp(jnp.sum(lv[0:1] * lv[1:2], axis=-1, keepdims=True))
           - jnp.exp(jnp.sum(lv[2:3] * lv[3:4], axis=-1, keepdims=True)) + LAMBDA_INIT)
    for x, (h, u) in enumerate(units):
        acc = acc_ref[x]
        o1 = acc[:V_DIM, :t] / acc[V_DIM:V_DIM + 1, :t]
        o2 = acc[:V_DIM, t:] / acc[V_DIM:V_DIM + 1, t:]
        o = o1 - lam * o2
        ms = jnp.mean(o * o, axis=0, keepdims=True)
        y = (o * lax.rsqrt(ms + NORM_EPS)).T
        o_ref[0, u * t:(u + 1) * t, h * V_DIM:(h + 1) * V_DIM] = (
            y * g_ref[...] * (1.0 - LAMBDA_INIT)).astype(o_ref.dtype)


def _attention(rel_bias, lamv, subln_g, qt, k, vt, bias):
    b, _, s = qt.shape
    t = ATTN_TILE
    tq = ATTN_SUBTILES * t
    n_chunks = s // ATTN_KEYS
    assert n_chunks >= ATTN_SUBTILES // 2 + 3
    n_units = N_HEADS * ATTN_SUBTILES
    smem = pl.BlockSpec(memory_space=pltpu.SMEM)
    return pl.pallas_call(
        functools.partial(_attn_kernel, n_chunks=n_chunks),
        grid=(b, s // tq),
        in_specs=[
            smem,
            pl.BlockSpec((4, HEAD_DIM), lambda bi, qi: (0, 0)),
            pl.BlockSpec((1, V_DIM), lambda bi, qi: (0, 0)),
            pl.BlockSpec((1, D_ATTN, tq), lambda bi, qi: (bi, 0, qi)),
            pl.BlockSpec((1, s, D_ATTN), lambda bi, qi: (bi, 0, 0)),
            pl.BlockSpec((1, s // t, D_ATTN, t), lambda bi, qi: (bi, 0, 0, 0)),
            pl.BlockSpec((N_HEADS, 3, t, t), lambda bi, qi: (0, 0, 0, 0)),
        ],
        out_specs=pl.BlockSpec((1, tq, D_ATTN), lambda bi, qi: (bi, qi, 0)),
        out_shape=jax.ShapeDtypeStruct((b, s, D_ATTN), jnp.bfloat16),
        scratch_shapes=[pltpu.VMEM((n_units, V_DIM, 2 * t), jnp.bfloat16),
                        pltpu.VMEM((ATTN_RING, ATTN_KEYS, 2 * t), jnp.float32),
                        pltpu.VMEM((n_units, 1, 2 * t), jnp.float32),
                        pltpu.VMEM((n_units, V_DIM + ONES_ROWS, 2 * t), jnp.float32)],
        compiler_params=pltpu.CompilerParams(
            dimension_semantics=("arbitrary", "arbitrary"),
            vmem_limit_bytes=VMEM_LIMIT),
        name="diff_attention",
    )(rel_bias, lamv, subln_g, qt, k, vt, bias)


def _lru_kernel(xr_ref, gr_ref, cw_ref, cb_ref, wg_ref, bg_ref, lam_ref, o_ref,
                xp_ref, af_ref, uf_ref, ab_ref, ub_ref, hfo_ref, hbo_ref, *, seq):
    cb = LRU_CHANNELS
    pad = 8
    xp_ref[0:pad, :] = jnp.zeros((pad, cb), jnp.float32)
    xp_ref[pad + seq:2 * pad + seq, :] = jnp.zeros((pad, cb), jnp.float32)
    xp_ref[pad:pad + seq, :] = xr_ref[0]

    cw = cw_ref[...]
    cbias = cb_ref[...]
    decay = jnp.log1p(jnp.exp(-lam_ref[...])) * (-LRU_C * LOG2E)
    n = LRU_CHUNK
    n_seg = seq // n
    groups = cb // LANES

    def seg_rows(ci):
        return pl.ds(ci, n, stride=n_seg)

    def gate_chunk(ci, carry):
        t0 = pl.multiple_of(ci * n, n)
        blk = xp_ref[pl.ds(t0, n + 2 * pad), :]
        xc = cbias
        for tap in range(CONV_WIDTH):
            lo = pad - CONV_WIDTH // 2 + tap
            xc = xc + blk[lo:lo + n] * cw[tap:tap + 1]
        for g in range(groups):
            cols = slice(g * LANES, (g + 1) * LANES)
            xg = xc[:, cols]
            gates = _dot(xg.astype(jnp.bfloat16), wg_ref[g]) + bg_ref[g]
            for direction, (a_ref, u_ref) in enumerate(((af_ref, uf_ref), (ab_ref, ub_ref))):
                r = jax.nn.sigmoid(gates[:, (2 * direction) * LANES:(2 * direction + 1) * LANES])
                ig = jax.nn.sigmoid(gates[:, (2 * direction + 1) * LANES:(2 * direction + 2) * LANES])
                a = jnp.exp2(r * decay[direction:direction + 1, cols])
                a_ref[g, seg_rows(ci), :] = a
                u_ref[g, seg_rows(ci), :] = jnp.sqrt(1.0 - a * a) * (ig * xg)
        return carry

    lax.fori_loop(0, n_seg, gate_chunk, 0)

    def tiles(k):
        rows_f = pl.ds(pl.multiple_of(k * n_seg, n_seg), n_seg)
        rows_b = pl.ds(pl.multiple_of((n - 1 - k) * n_seg, n_seg), n_seg)
        return rows_f, rows_b

    def ends_body(k, carry):
        rows_f, rows_b = tiles(k)
        out = []
        for g in range(groups):
            hf, pf, hb, pb = carry[g]
            a = af_ref[g, rows_f, :]
            hf = a * hf + uf_ref[g, rows_f, :]
            pf = a * pf
            a = ab_ref[g, rows_b, :]
            hb = a * hb + ub_ref[g, rows_b, :]
            pb = a * pb
            out.append((hf, pf, hb, pb))
        return tuple(out)

    zero = jnp.zeros((n_seg, LANES), jnp.float32)
    one = jnp.ones((n_seg, LANES), jnp.float32)
    ends = lax.fori_loop(0, n, ends_body, tuple((zero, one, zero, one) for _ in range(groups)),
                         unroll=LRU_UNROLL)

    starts = []
    for g in range(groups):
        hf, pf, hb, pb = ends[g]
        state = jnp.zeros((1, LANES), jnp.float32)
        fwd = [state]
        for r in range(1, n_seg):
            state = hf[r - 1:r] + pf[r - 1:r] * state
            fwd.append(state)
        state = jnp.zeros((1, LANES), jnp.float32)
        bwd = [state]
        for r in range(n_seg - 2, -1, -1):
            state = hb[r + 1:r + 2] + pb[r + 1:r + 2] * state
            bwd.insert(0, state)
        starts.append((jnp.concatenate(fwd, axis=0), jnp.concatenate(bwd, axis=0)))

    def scan_body(k, carry):
        rows_f, rows_b = tiles(k)
        out = []
        for g in range(groups):
            hf, hb = carry[g]
            hf = af_ref[g, rows_f, :] * hf + uf_ref[g, rows_f, :]
            hfo_ref[g, rows_f, :] = hf
            hb = ab_ref[g, rows_b, :] * hb + ub_ref[g, rows_b, :]
            hbo_ref[g, rows_b, :] = hb
            out.append((hf, hb))
        return tuple(out)

    lax.fori_loop(0, n, scan_body, tuple(starts), unroll=LRU_UNROLL)

    def out_chunk(ci, carry):
        t0 = pl.multiple_of(ci * n, n)
        for g in range(groups):
            cols = slice(g * LANES, (g + 1) * LANES)
            gate = jax.nn.gelu(gr_ref[0, pl.ds(t0, n), cols])
            o_ref[0, pl.ds(t0, n), cols] = (
                gate * (hfo_ref[g, seg_rows(ci), :] + hbo_ref[g, seg_rows(ci), :])).astype(o_ref.dtype)
        return carry

    lax.fori_loop(0, n_seg, out_chunk, 0)


def _rglru(xr, gr, conv_w, conv_b, wg, bg, lru_lambda):
    b, s, c = xr.shape
    cb = LRU_CHANNELS
    groups = cb // LANES
    seq_spec = pl.BlockSpec((1, s, cb), lambda bi, ci: (bi, 0, ci))
    return pl.pallas_call(
        functools.partial(_lru_kernel, seq=s),
        grid=(b, c // cb),
        in_specs=[
            seq_spec, seq_spec,
            pl.BlockSpec((CONV_WIDTH, cb), lambda bi, ci: (0, ci)),
            pl.BlockSpec((1, cb), lambda bi, ci: (0, ci)),
            pl.BlockSpec((groups, LANES, 4 * LANES), lambda bi, ci: (ci, 0, 0)),
            pl.BlockSpec((groups, 1, 4 * LANES), lambda bi, ci: (ci, 0, 0)),
            pl.BlockSpec((2, cb), lambda bi, ci: (0, ci)),
        ],
        out_specs=seq_spec,
        out_shape=jax.ShapeDtypeStruct((b, s, c), jnp.bfloat16),
        scratch_shapes=[pltpu.VMEM((s + 16, cb), jnp.float32)]
        + [pltpu.VMEM((groups, s, LANES), jnp.float32) for _ in range(6)],
        compiler_params=pltpu.CompilerParams(
            dimension_semantics=("arbitrary", "arbitrary"), vmem_limit_bytes=VMEM_LIMIT),
        name="rglru",
    )(xr, gr, conv_w, conv_b, wg, bg, lru_lambda)


def _lru_gate_weights(w_rg, b_rg, w_ig, b_ig):
    n_groups = D_LRU // LANES
    per = LANES // LRU_BLOCK

    def blockdiag(w):
        w = w.reshape(n_groups, per, LRU_BLOCK, LRU_BLOCK)
        eye = jnp.eye(per, dtype=w.dtype)
        return jnp.einsum('gpde,pq->gpdqe', w, eye).reshape(n_groups, LANES, LANES)

    wg = jnp.concatenate([blockdiag(w_rg[0]), blockdiag(w_ig[0]),
                          blockdiag(w_rg[1]), blockdiag(w_ig[1])], axis=-1)
    bg = jnp.concatenate([b.reshape(n_groups, 1, LANES) for b in (b_rg[0], b_ig[0], b_rg[1], b_ig[1])],
                         axis=-1)
    return wg.astype(jnp.bfloat16), bg


def _ffn_kernel(x_ref, ya_ref, yl_ref, wo_ref, g2_ref, wgate_ref, wup_ref, wdown_ref, gf_ref, o_ref,
                act_ref):
    part = FFN_PART_ROWS
    n_parts = x_ref.shape[0] // part
    rows = [slice(i * part, (i + 1) * part) for i in range(n_parts)]

    def out_proj(r):
        return (x_ref[r, :] + _dot(ya_ref[r, :], wo_ref[:D_ATTN, :])
                + _dot(yl_ref[r, :], wo_ref[D_ATTN:, :]))

    def normed(h, g_ref):
        ms = jnp.mean(h * h, axis=-1, keepdims=True)
        return h * lax.rsqrt(ms + NORM_EPS) * g_ref[...]

    def swiglu(r, n2):
        for c in range(D_FF // FFN_CHUNK):
            cols = slice(c * FFN_CHUNK, (c + 1) * FFN_CHUNK)
            gate = _dot(n2, wgate_ref[:, cols])
            up = _dot(n2, wup_ref[:, cols])
            act_ref[r, cols] = (gate * jax.nn.sigmoid(gate) * up).astype(jnp.bfloat16)

    h1 = [out_proj(r) for r in rows]
    n2 = normed(h1[0], g2_ref).astype(jnp.bfloat16)
    for i in range(n_parts):
        swiglu(rows[i], n2)
        if i + 1 < n_parts:
            n2 = normed(h1[i + 1], g2_ref).astype(jnp.bfloat16)
        if i > 0:
            o_ref[rows[i - 1], :] = normed(h2, gf_ref)
        h2 = h1[i] + _dot(act_ref[rows[i], :], wdown_ref[...])
    o_ref[rows[-1], :] = normed(h2, gf_ref)


def _outproj_ffn(x2, ya, yl, wo, g2, wgate, wup, wdown, gf, tm=1024):
    n_tok, d = x2.shape
    once = lambda shape: pl.BlockSpec(shape, lambda ti: (0,) * len(shape),
                                      pipeline_mode=pl.Buffered(1))
    return pl.pallas_call(
        _ffn_kernel,
        grid=(n_tok // tm,),
        in_specs=[
            pl.BlockSpec((tm, d), lambda ti: (ti, 0)),
            pl.BlockSpec((tm, D_ATTN), lambda ti: (ti, 0)),
            pl.BlockSpec((tm, D_LRU), lambda ti: (ti, 0)),
            once((d, d)), once((1, d)),
            once((d, D_FF)), once((d, D_FF)), once((D_FF, d)), once((1, d)),
        ],
        out_specs=pl.BlockSpec((tm, d), lambda ti: (ti, 0)),
        out_shape=jax.ShapeDtypeStruct((n_tok, d), jnp.float32),
        scratch_shapes=[pltpu.VMEM((tm, D_FF), jnp.bfloat16)],
        compiler_params=pltpu.CompilerParams(
            dimension_semantics=("arbitrary",), vmem_limit_bytes=VMEM_LIMIT),
        name="outproj_ffn",
    )(x2, ya, yl, wo, g2, wgate, wup, wdown, gf)


def kernel(x, attn_norm_g, w_in, lambda_q1, lambda_k1, lambda_q2, lambda_k2, subln_g, rel_bias,
           conv_w, conv_b, w_rg, b_rg, w_ig, b_ig, lru_lambda, w_out, ffn_norm_g, w_gate, w_up,
           w_down, final_norm_g):
    b, s, d = x.shape
    bf16 = jnp.bfloat16
    w = w_in[0]
    wqt = w[:, :D_ATTN].T.astype(bf16)
    wk = w[:, D_ATTN:2 * D_ATTN].astype(bf16)
    wvt = w[:, 2 * D_ATTN:3 * D_ATTN].T.astype(bf16)
    wx = w[:, 3 * D_ATTN:3 * D_ATTN + D_LRU].astype(bf16)
    wgr = w[:, 3 * D_ATTN + D_LRU:].astype(bf16)

    qt, k, vt, xr, gr = _inproj(x, attn_norm_g[0][None, :], wqt, wk, wvt, wx, wgr)

    bias = _bias_tiles(rel_bias)
    lamv = jnp.stack([lambda_q1[0], lambda_k1[0], lambda_q2[0], lambda_k2[0]])
    y_attn = _attention(rel_bias, lamv, subln_g[0][None, :], qt, k, vt, bias)

    wg, bg = _lru_gate_weights(w_rg[0], b_rg[0], w_ig[0], b_ig[0])
    y_lru = _rglru(xr, gr, conv_w[0], conv_b[0][None, :], wg, bg, lru_lambda[0])

    out = _outproj_ffn(
        x.reshape(b * s, d), y_attn.reshape(b * s, D_ATTN), y_lru.reshape(b * s, D_LRU),
        w_out[0].astype(bf16), ffn_norm_g[0][None, :], w_gate[0].astype(bf16),
        w_up[0].astype(bf16), w_down[0].astype(bf16), final_norm_g[None, :])
    return out.reshape(b, s, d)
```

```python
import functools
import math

import jax
import jax.numpy as jnp
from jax import lax
from jax.experimental import pallas as pl
from jax.experimental.pallas import tpu as pltpu

D_MODEL = 1024
D_ATTN = 512
D_LRU = 512
HEAD_DIM = 64
N_HEADS = 4
V_DIM = 128
NUM_BUCKETS = 32
LRU_BLOCK = 64
LRU_C = 8.0
CONV_WIDTH = 4
D_FF = 2816
NORM_EPS = 1e-6
LAMBDA_INIT = 0.2
LOG2E = 1.4426950408889634

LANES = 128
ATTN_TILE = 256
ATTN_KEYS = 512
ATTN_SUBTILES = 4
ATTN_LEAD = 2
ATTN_RING = 8
ONES_ROWS = 16
LRU_CHANNELS = 256
LRU_CHUNK = 512
LRU_UNROLL = 8
FFN_CHUNK = 256
FFN_PART_ROWS = 256
VMEM_LIMIT = 56 * 1024 * 1024

_BUCKET_THRESHOLDS = (12, 16, 23, 32, 46, 64, 91)
assert _BUCKET_THRESHOLDS[-1] <= ATTN_TILE


def _nt_dot(a, b):
    return lax.dot_general(a, b, (((1,), (1,)), ((), ())), preferred_element_type=jnp.float32)


def _dot(a, b):
    return jnp.dot(a, b, preferred_element_type=jnp.float32)


def _bias_kernel(table_ref, out_ref):
    h = pl.program_id(0)
    d = pl.program_id(1)
    t = ATTN_TILE
    kk = lax.broadcasted_iota(jnp.int32, (t, t), 0)
    qq = lax.broadcasted_iota(jnp.int32, (t, t), 1)
    rel = (d - 1) * t + kk - qq
    n = jnp.abs(rel)
    large = jnp.full((t, t), NUM_BUCKETS // 4, jnp.int32)
    for thr in _BUCKET_THRESHOLDS:
        large = large + (n >= thr).astype(jnp.int32)
    bucket = jnp.where(rel > 0, NUM_BUCKETS // 2, 0) + jnp.where(n < NUM_BUCKETS // 4, n, large)
    acc = jnp.zeros((t, t), jnp.float32)
    for b in range(NUM_BUCKETS):
        acc = jnp.where(bucket == b, table_ref[b, h], acc)
    out_ref[0, 0] = acc * LOG2E


def _bias_tiles(rel_bias):
    t = ATTN_TILE
    return pl.pallas_call(
        _bias_kernel,
        grid=(N_HEADS, 3),
        in_specs=[pl.BlockSpec(memory_space=pltpu.SMEM)],
        out_specs=pl.BlockSpec((1, 1, t, t), lambda h, d: (h, d, 0, 0)),
        out_shape=jax.ShapeDtypeStruct((N_HEADS, 3, t, t), jnp.float32),
        name="bias_tiles",
    )(rel_bias)


def _inproj_kernel(x_ref, g_ref, wqt_ref, wk_ref, wvt_ref, wx_ref, wg_ref,
                   qt_ref, k_ref, vt_ref, xr_ref, gr_ref):
    half = x_ref.shape[1] // 2
    cph = half // ATTN_TILE

    def normed(r):
        x = x_ref[0, r, :]
        ms = jnp.mean(x * x, axis=-1, keepdims=True)
        return (x * lax.rsqrt(ms + NORM_EPS) * g_ref[...]).astype(jnp.bfloat16)

    def attn_proj(i, r, n):
        qt = _nt_dot(wqt_ref[...], n)
        qt_ref[0, :, r] = (qt * (LOG2E / math.sqrt(HEAD_DIM))).astype(jnp.bfloat16)
        k_ref[0, r, :] = _dot(n, wk_ref[...]).astype(jnp.bfloat16)

    def rest_proj(i, r, n):
        vt = _nt_dot(wvt_ref[...], n).astype(jnp.bfloat16)
        for c in range(cph):
            vt_ref[0, i * cph + c] = vt[:, c * ATTN_TILE:(c + 1) * ATTN_TILE]
        xr_ref[0, r, :] = _dot(n, wx_ref[...])
        gr_ref[0, r, :] = _dot(n, wg_ref[...])

    rows = [slice(0, half), slice(half, 2 * half)]
    n_a = normed(rows[0])
    attn_proj(0, rows[0], n_a)
    n_b = normed(rows[1])
    rest_proj(0, rows[0], n_a)
    attn_proj(1, rows[1], n_b)
    rest_proj(1, rows[1], n_b)


def _inproj(x, g, wqt, wk, wvt, wx, wg, tm=1024):
    b, s, d = x.shape
    nc = s // ATTN_TILE
    cpt = tm // ATTN_TILE
    const = lambda shape: pl.BlockSpec(shape, lambda bi, ti: (0,) * len(shape))
    return pl.pallas_call(
        _inproj_kernel,
        grid=(b, s // tm),
        in_specs=[
            pl.BlockSpec((1, tm, d), lambda bi, ti: (bi, ti, 0)),
            const((1, d)),
            const((D_ATTN, d)), const((d, D_ATTN)), const((D_ATTN, d)),
            const((d, D_LRU)), const((d, D_LRU)),
        ],
        out_specs=[
            pl.BlockSpec((1, D_ATTN, tm), lambda bi, ti: (bi, 0, ti)),
            pl.BlockSpec((1, tm, D_ATTN), lambda bi, ti: (bi, ti, 0)),
            pl.BlockSpec((1, cpt, D_ATTN, ATTN_TILE), lambda bi, ti: (bi, ti, 0, 0)),
            pl.BlockSpec((1, tm, D_LRU), lambda bi, ti: (bi, ti, 0)),
            pl.BlockSpec((1, tm, D_LRU), lambda bi, ti: (bi, ti, 0)),
        ],
        out_shape=[
            jax.ShapeDtypeStruct((b, D_ATTN, s), jnp.bfloat16),
            jax.ShapeDtypeStruct((b, s, D_ATTN), jnp.bfloat16),
            jax.ShapeDtypeStruct((b, nc, D_ATTN, ATTN_TILE), jnp.bfloat16),
            jax.ShapeDtypeStruct((b, s, D_LRU), jnp.float32),
            jax.ShapeDtypeStruct((b, s, D_LRU), jnp.float32),
        ],
        compiler_params=pltpu.CompilerParams(
            dimension_semantics=("arbitrary", "arbitrary"), vmem_limit_bytes=VMEM_LIMIT),
        name="inproj",
    )(x, g, wqt, wk, wvt, wx, wg)


def _attn_kernel(table_ref, lamv_ref, g_ref, qt_ref, k_ref, vt_ref, bias_ref, o_ref,
                 qz_ref, s_ref, m_ref, acc_ref, *, n_chunks):
    t = ATTN_TILE
    kc = ATTN_KEYS
    ip = pl.program_id(1)
    units = [(h, u) for h in range(N_HEADS) for u in range(ATTN_SUBTILES)]
    n_units = len(units)
    ring = ATTN_RING
    assert n_units % ring == 0 and ring > ATTN_LEAD and ATTN_SUBTILES % 2 == 0

    z = jnp.zeros((HEAD_DIM, t), jnp.bfloat16)
    for x, (h, u) in enumerate(units):
        qt = qt_ref[0, h * V_DIM:(h + 1) * V_DIM, u * t:(u + 1) * t]
        qz_ref[x] = jnp.concatenate([jnp.concatenate([qt[:HEAD_DIM], z], axis=1),
                                     jnp.concatenate([z, qt[HEAD_DIM:]], axis=1)], axis=0)
    ones = jnp.ones((ONES_ROWS, kc), jnp.bfloat16)
    m_ref[...] = jnp.full(m_ref.shape, -1e30, jnp.float32)
    acc_ref[...] = jnp.zeros(acc_ref.shape, jnp.float32)

    c_left = [table_ref[NUM_BUCKETS // 2 - 1, h] * LOG2E for h in range(N_HEADS)]
    c_right = [table_ref[NUM_BUCKETS - 1, h] * LOG2E for h in range(N_HEADS)]

    def scores(j, x):
        h, _ = units[x]
        rows = pl.ds(pl.multiple_of(j * kc, kc), kc)
        s_ref[x % ring] = _dot(k_ref[0, rows, h * V_DIM:(h + 1) * V_DIM], qz_ref[x])

    def accumulate(j, x, rel, on_left):
        h, u = units[x]
        s = s_ref[x % ring]
        if rel is None:
            const = jnp.where(on_left, c_left[h], c_right[h])
        else:
            halves = [2 * rel - u, 2 * rel - u + 1]
            if halves[1] < -1:
                const = c_left[h]
            elif halves[0] > 1:
                const = c_right[h]
            else:
                const = None
                parts = [bias_ref[h, d + 1] if abs(d) <= 1
                         else jnp.full((t, t), c_left[h] if d < 0 else c_right[h], jnp.float32)
                         for d in halves]
                bias = jnp.concatenate(parts, axis=0)
                s = s + jnp.concatenate([bias, bias], axis=1)
        m_old = m_ref[x]
        m_cur = jnp.max(s, axis=0, keepdims=True)
        if const is not None:
            m_cur = m_cur + const
        m_new = jnp.maximum(m_old, m_cur)
        alpha = jnp.exp2(m_old - m_new)
        shift = m_new if const is None else m_new - const
        p = jnp.exp2(s - shift).astype(jnp.bfloat16)
        hd = slice(h * V_DIM, (h + 1) * V_DIM)
        vta = jnp.concatenate([jnp.concatenate([vt_ref[0, 2 * j, hd, :], vt_ref[0, 2 * j + 1, hd, :]], axis=1),
                               ones], axis=0)
        acc_ref[x] = acc_ref[x] * alpha + _dot(vta, p)
        m_ref[x] = m_new

    def chunk_body(j, j_next, rel, on_left=None, last=False):
        for x in range(n_units):
            ahead = x + ATTN_LEAD
            if ahead < n_units:
                scores(j, ahead)
            elif not last:
                scores(j_next, ahead - n_units)
            accumulate(j, x, rel, on_left)

    base = (ATTN_SUBTILES // 2) * ip
    near = (-1, ATTN_SUBTILES // 2)
    lo = jnp.maximum(base + near[0], 0)
    hi = jnp.minimum(base + near[1], n_chunks - 1)
    n_left = lo
    n_far = n_left + (n_chunks - 1 - hi)

    def far_chunk(i):
        return jnp.where(i < n_left, i, i - n_left + hi + 1)

    for x in range(ATTN_LEAD):
        scores(lo, x)

    for rel in range(near[0], near[1] + 1):
        j = base + rel

        @pl.when(jnp.logical_and(j >= lo, j <= hi))
        def _():
            chunk_body(j, jnp.where(j < hi, j + 1, far_chunk(0)), rel)

    def far_loop(i, carry):
        j = far_chunk(i)
        chunk_body(j, far_chunk(i + 1), None, on_left=j < lo)
        return carry

    lax.fori_loop(0, n_far - 1, far_loop, 0)
    j_last = far_chunk(n_far - 1)
    chunk_body(j_last, None, None, on_left=j_last < lo, last=True)

    lv = lamv_ref[...]
    lam = (jnp.exp(jnp.sum(lv[0:1] * lv[1:2], axis=-1, keepdims=True))
           - jnp.exp(jnp.sum(lv[2:3] * lv[3:4], axis=-1, keepdims=True)) + LAMBDA_INIT)
    for x, (h, u) in enumerate(units):
        acc = acc_ref[x]
        o1 = acc[:V_DIM, :t] / acc[V_DIM:V_DIM + 1, :t]
        o2 = acc[:V_DIM, t:] / acc[V_DIM:V_DIM + 1, t:]
        o = o1 - lam * o2
        ms = jnp.mean(o * o, axis=0, keepdims=True)
        y = (o * lax.rsqrt(ms + NORM_EPS)).T
        o_ref[0, u * t:(u + 1) * t, h * V_DIM:(h + 1) * V_DIM] = (
            y * g_ref[...] * (1.0 - LAMBDA_INIT)).astype(o_ref.dtype)


def _attention(rel_bias, lamv, subln_g, qt, k, vt, bias):
    b, _, s = qt.shape
    t = ATTN_TILE
    tq = ATTN_SUBTILES * t
    n_chunks = s // ATTN_KEYS
    assert n_chunks >= ATTN_SUBTILES // 2 + 3
    n_units = N_HEADS * ATTN_SUBTILES
    smem = pl.BlockSpec(memory_space=pltpu.SMEM)
    return pl.pallas_call(
        functools.partial(_attn_kernel, n_chunks=n_chunks),
        grid=(b, s // tq),
        in_specs=[
            smem,
            pl.BlockSpec((4, HEAD_DIM), lambda bi, qi: (0, 0)),
            pl.BlockSpec((1, V_DIM), lambda bi, qi: (0, 0)),
            pl.BlockSpec((1, D_ATTN, tq), lambda bi, qi: (bi, 0, qi)),
            pl.BlockSpec((1, s, D_ATTN), lambda bi, qi: (bi, 0, 0)),
            pl.BlockSpec((1, s // t, D_ATTN, t), lambda bi, qi: (bi, 0, 0, 0)),
            pl.BlockSpec((N_HEADS, 3, t, t), lambda bi, qi: (0, 0, 0, 0)),
        ],
        out_specs=pl.BlockSpec((1, tq, D_ATTN), lambda bi, qi: (bi, qi, 0)),
        out_shape=jax.ShapeDtypeStruct((b, s, D_ATTN), jnp.bfloat16),
        scratch_shapes=[pltpu.VMEM((n_units, V_DIM, 2 * t), jnp.bfloat16),
                        pltpu.VMEM((ATTN_RING, ATTN_KEYS, 2 * t), jnp.float32),
                        pltpu.VMEM((n_units, 1, 2 * t), jnp.float32),
                        pltpu.VMEM((n_units, V_DIM + ONES_ROWS, 2 * t), jnp.float32)],
        compiler_params=pltpu.CompilerParams(
            dimension_semantics=("arbitrary", "arbitrary"),
            vmem_limit_bytes=VMEM_LIMIT),
        name="diff_attention",
    )(rel_bias, lamv, subln_g, qt, k, vt, bias)


def _lru_kernel(xr_ref, gr_ref, cw_ref, cb_ref, wg_ref, bg_ref, lam_ref, o_ref,
                xp_ref, af_ref, uf_ref, ab_ref, ub_ref, hfo_ref, hbo_ref, *, seq):
    cb = LRU_CHANNELS
    pad = 8
    xp_ref[0:pad, :] = jnp.zeros((pad, cb), jnp.float32)
    xp_ref[pad + seq:2 * pad + seq, :] = jnp.zeros((pad, cb), jnp.float32)
    xp_ref[pad:pad + seq, :] = xr_ref[0]

    cw = cw_ref[...]
    cbias = cb_ref[...]
    decay = jnp.log1p(jnp.exp(-lam_ref[...])) * (-LRU_C * LOG2E)
    n = LRU_CHUNK
    n_seg = seq // n
    groups = cb // LANES

    def seg_rows(ci):
        return pl.ds(ci, n, stride=n_seg)

    def gate_chunk(ci, carry):
        t0 = pl.multiple_of(ci * n, n)
        blk = xp_ref[pl.ds(t0, n + 2 * pad), :]
        xc = cbias
        for tap in range(CONV_WIDTH):
            lo = pad - CONV_WIDTH // 2 + tap
            xc = xc + blk[lo:lo + n] * cw[tap:tap + 1]
        for g in range(groups):
            cols = slice(g * LANES, (g + 1) * LANES)
            xg = xc[:, cols]
            gates = _dot(xg.astype(jnp.bfloat16), wg_ref[g]) + bg_ref[g]
            for direction, (a_ref, u_ref) in enumerate(((af_ref, uf_ref), (ab_ref, ub_ref))):
                r = jax.nn.sigmoid(gates[:, (2 * direction) * LANES:(2 * direction + 1) * LANES])
                ig = jax.nn.sigmoid(gates[:, (2 * direction + 1) * LANES:(2 * direction + 2) * LANES])
                a = jnp.exp2(r * decay[direction:direction + 1, cols])
                a_ref[g, seg_rows(ci), :] = a
                u_ref[g, seg_rows(ci), :] = jnp.sqrt(1.0 - a * a) * (ig * xg)
        return carry

    lax.fori_loop(0, n_seg, gate_chunk, 0)

    def tiles(k):
        rows_f = pl.ds(pl.multiple_of(k * n_seg, n_seg), n_seg)
        rows_b = pl.ds(pl.multiple_of((n - 1 - k) * n_seg, n_seg), n_seg)
        return rows_f, rows_b

    def ends_body(k, carry):
        rows_f, rows_b = tiles(k)
        out = []
        for g in range(groups):
            hf, pf, hb, pb = carry[g]
            a = af_ref[g, rows_f, :]
            hf = a * hf + uf_ref[g, rows_f, :]
            pf = a * pf
            a = ab_ref[g, rows_b, :]
            hb = a * hb + ub_ref[g, rows_b, :]
            pb = a * pb
            out.append((hf, pf, hb, pb))
        return tuple(out)

    zero = jnp.zeros((n_seg, LANES), jnp.float32)
    one = jnp.ones((n_seg, LANES), jnp.float32)
    ends = lax.fori_loop(0, n, ends_body, tuple((zero, one, zero, one) for _ in range(groups)),
                         unroll=LRU_UNROLL)

    starts = []
    for g in range(groups):
        hf, pf, hb, pb = ends[g]
        state = jnp.zeros((1, LANES), jnp.float32)
        fwd = [state]
        for r in range(1, n_seg):
            state = hf[r - 1:r] + pf[r - 1:r] * state
            fwd.append(state)
        state = jnp.zeros((1, LANES), jnp.float32)
        bwd = [state]
        for r in range(n_seg - 2, -1, -1):
            state = hb[r + 1:r + 2] + pb[r + 1:r + 2] * state
            bwd.insert(0, state)
        starts.append((jnp.concatenate(fwd, axis=0), jnp.concatenate(bwd, axis=0)))

    def scan_body(k, carry):
        rows_f, rows_b = tiles(k)
        out = []
        for g in range(groups):
            hf, hb = carry[g]
            hf = af_ref[g, rows_f, :] * hf + uf_ref[g, rows_f, :]
            hfo_ref[g, rows_f, :] = hf
            hb = ab_ref[g, rows_b, :] * hb + ub_ref[g, rows_b, :]
            hbo_ref[g, rows_b, :] = hb
            out.append((hf, hb))
        return tuple(out)

    lax.fori_loop(0, n, scan_body, tuple(starts), unroll=LRU_UNROLL)

    def out_chunk(ci, carry):
        t0 = pl.multiple_of(ci * n, n)
        for g in range(groups):
            cols = slice(g * LANES, (g + 1) * LANES)
            gate = jax.nn.gelu(gr_ref[0, pl.ds(t0, n), cols])
            o_ref[0, pl.ds(t0, n), cols] = (
                gate * (hfo_ref[g, seg_rows(ci), :] + hbo_ref[g, seg_rows(ci), :])).astype(o_ref.dtype)
        return carry

    lax.fori_loop(0, n_seg, out_chunk, 0)


def _rglru(xr, gr, conv_w, conv_b, wg, bg, lru_lambda):
    b, s, c = xr.shape
    cb = LRU_CHANNELS
    groups = cb // LANES
    seq_spec = pl.BlockSpec((1, s, cb), lambda bi, ci: (bi, 0, ci))
    return pl.pallas_call(
        functools.partial(_lru_kernel, seq=s),
        grid=(b, c // cb),
        in_specs=[
            seq_spec, seq_spec,
            pl.BlockSpec((CONV_WIDTH, cb), lambda bi, ci: (0, ci)),
            pl.BlockSpec((1, cb), lambda bi, ci: (0, ci)),
            pl.BlockSpec((groups, LANES, 4 * LANES), lambda bi, ci: (ci, 0, 0)),
            pl.BlockSpec((groups, 1, 4 * LANES), lambda bi, ci: (ci, 0, 0)),
            pl.BlockSpec((2, cb), lambda bi, ci: (0, ci)),
        ],
        out_specs=seq_spec,
        out_shape=jax.ShapeDtypeStruct((b, s, c), jnp.bfloat16),
        scratch_shapes=[pltpu.VMEM((s + 16, cb), jnp.float32)]
        + [pltpu.VMEM((groups, s, LANES), jnp.float32) for _ in range(6)],
        compiler_params=pltpu.CompilerParams(
            dimension_semantics=("arbitrary", "arbitrary"), vmem_limit_bytes=VMEM_LIMIT),
        name="rglru",
    )(xr, gr, conv_w, conv_b, wg, bg, lru_lambda)


def _lru_gate_weights(w_rg, b_rg, w_ig, b_ig):
    n_groups = D_LRU // LANES
    per = LANES // LRU_BLOCK

    def blockdiag(w):
        w = w.reshape(n_groups, per, LRU_BLOCK, LRU_BLOCK)
        eye = jnp.eye(per, dtype=w.dtype)
        return jnp.einsum('gpde,pq->gpdqe', w, eye).reshape(n_groups, LANES, LANES)

    wg = jnp.concatenate([blockdiag(w_rg[0]), blockdiag(w_ig[0]),
                          blockdiag(w_rg[1]), blockdiag(w_ig[1])], axis=-1)
    bg = jnp.concatenate([b.reshape(n_groups, 1, LANES) for b in (b_rg[0], b_ig[0], b_rg[1], b_ig[1])],
                         axis=-1)
    return wg.astype(jnp.bfloat16), bg


def _ffn_kernel(x_ref, ya_ref, yl_ref, wo_ref, g2_ref, wgate_ref, wup_ref, wdown_ref, gf_ref, o_ref,
                act_ref):
    part = FFN_PART_ROWS
    n_parts = x_ref.shape[0] // part
    rows = [slice(i * part, (i + 1) * part) for i in range(n_parts)]

    def out_proj(r):
        return (x_ref[r, :] + _dot(ya_ref[r, :], wo_ref[:D_ATTN, :])
                + _dot(yl_ref[r, :], wo_ref[D_ATTN:, :]))

    def normed(h, g_ref):
        ms = jnp.mean(h * h, axis=-1, keepdims=True)
        return h * lax.rsqrt(ms + NORM_EPS) * g_ref[...]

    def swiglu(r, n2):
        for c in range(D_FF // FFN_CHUNK):
            cols = slice(c * FFN_CHUNK, (c + 1) * FFN_CHUNK)
            gate = _dot(n2, wgate_ref[:, cols])
            up = _dot(n2, wup_ref[:, cols])
            act_ref[r, cols] = (gate * jax.nn.sigmoid(gate) * up).astype(jnp.bfloat16)

    h1 = [out_proj(r) for r in rows]
    n2 = normed(h1[0], g2_ref).astype(jnp.bfloat16)
    for i in range(n_parts):
        swiglu(rows[i], n2)
        if i + 1 < n_parts:
            n2 = normed(h1[i + 1], g2_ref).astype(jnp.bfloat16)
        if i > 0:
            o_ref[rows[i - 1], :] = normed(h2, gf_ref)
        h2 = h1[i] + _dot(act_ref[rows[i], :], wdown_ref[...])
    o_ref[rows[-1], :] = normed(h2, gf_ref)


def _outproj_ffn(x2, ya, yl, wo, g2, wgate, wup, wdown, gf, tm=1024):
    n_tok, d = x2.shape
    once = lambda shape: pl.BlockSpec(shape, lambda ti: (0,) * len(shape),
                                      pipeline_mode=pl.Buffered(1))
    return pl.pallas_call(
        _ffn_kernel,
        grid=(n_tok // tm,),
        in_specs=[
            pl.BlockSpec((tm, d), lambda ti: (ti, 0)),
            pl.BlockSpec((tm, D_ATTN), lambda ti: (ti, 0)),
            pl.BlockSpec((tm, D_LRU), lambda ti: (ti, 0)),
            once((d, d)), once((1, d)),
            once((d, D_FF)), once((d, D_FF)), once((D_FF, d)), once((1, d)),
        ],
        out_specs=pl.BlockSpec((tm, d), lambda ti: (ti, 0)),
        out_shape=jax.ShapeDtypeStruct((n_tok, d), jnp.float32),
        scratch_shapes=[pltpu.VMEM((tm, D_FF), jnp.bfloat16)],
        compiler_params=pltpu.CompilerParams(
            dimension_semantics=("arbitrary",), vmem_limit_bytes=VMEM_LIMIT),
        name="outproj_ffn",
    )(x2, ya, yl, wo, g2, wgate, wup, wdown, gf)


def kernel(x, attn_norm_g, w_in, lambda_q1, lambda_k1, lambda_q2, lambda_k2, subln_g, rel_bias,
           conv_w, conv_b, w_rg, b_rg, w_ig, b_ig, lru_lambda, w_out, ffn_norm_g, w_gate, w_up,
           w_down, final_norm_g):
    b, s, d = x.shape
    bf16 = jnp.bfloat16
    w = w_in[0]
    wqt = w[:, :D_ATTN].T.astype(bf16)
    wk = w[:, D_ATTN:2 * D_ATTN].astype(bf16)
    wvt = w[:, 2 * D_ATTN:3 * D_ATTN].T.astype(bf16)
    wx = w[:, 3 * D_ATTN:3 * D_ATTN + D_LRU].astype(bf16)
    wgr = w[:, 3 * D_ATTN + D_LRU:].astype(bf16)

    qt, k, vt, xr, gr = _inproj(x, attn_norm_g[0][None, :], wqt, wk, wvt, wx, wgr)

    bias = _bias_tiles(rel_bias)
    lamv = jnp.stack([lambda_q1[0], lambda_k1[0], lambda_q2[0], lambda_k2[0]])
    y_attn = _attention(rel_bias, lamv, subln_g[0][None, :], qt, k, vt, bias)

    wg, bg = _lru_gate_weights(w_rg[0], b_rg[0], w_ig[0], b_ig[0])
    y_lru = _rglru(xr, gr, conv_w[0], conv_b[0][None, :], wg, bg, lru_lambda[0])

    out = _outproj_ffn(
        x.reshape(b * s, d), y_attn.reshape(b * s, D_ATTN), y_lru.reshape(b * s, D_LRU),
        w_out[0].astype(bf16), ffn_norm_g[0][None, :], w_gate[0].astype(bf16),
        w_up[0].astype(bf16), w_down[0].astype(bf16), final_norm_g[None, :])
    return out.reshape(b, s, d)
```
